```python
import math
import jax, jax.numpy as jnp
from jax import lax
import numpy as np

D_MODEL = 1024
BATCH = 2
SEQ = 8192
DEPTH = 1

CHUNK = 64
D_MIX = D_MODEL
D_SSM = D_MIX // 2
D_ATT = D_MIX - D_SSM
SSM_GROUP = 16
N_SSM_GROUPS = D_SSM // SSM_GROUP
SSM_STATE = 64
HEAD_DIM = 64
N_HEADS = D_ATT // HEAD_DIM
LEFT_CHUNKS = 8
BAND = (LEFT_CHUNKS + 1) * CHUNK
MAX_REL = 128
D_FF = -(-8 * D_MODEL // (3 * 256)) * 256
D_IN = D_SSM + 3 * D_ATT
EPS = 1e-6
DT_MIN = 1e-3
DT_MAX = 1e-1

kernel_name = "hymba_s5_chunked_attention_block"


def rmsnorm(x, g):
    xf = x.astype(jnp.float32)
    y = xf * lax.rsqrt(jnp.mean(xf * xf, axis=-1, keepdims=True) + EPS)
    return (y * g.astype(jnp.float32)).astype(x.dtype)


def s5_mixer(u, lam_re, lam_im, log_dt, b_re, b_im, c_re, c_im, d_skip, w_glu, b_glu):
    bsz, seq, _ = u.shape
    f32 = jnp.float32
    uf = u.astype(f32).reshape(bsz, seq, N_SSM_GROUPS, SSM_GROUP)
    lam = lax.complex(lam_re.astype(f32), lam_im.astype(f32))
    dt = jnp.exp(log_dt.astype(f32))[:, None]
    lam_bar = jnp.exp(lam * dt)
    b_mat = lax.complex(b_re.astype(f32), b_im.astype(f32))
    b_bar = ((lam_bar - 1.0) / lam)[..., None] * b_mat
    bu = jnp.einsum('bsgh,gph->bsgp', uf.astype(jnp.complex64), b_bar)
    a = jnp.broadcast_to(lam_bar, bu.shape)

    def combine(left, right):
        a_l, b_l = left
        a_r, b_r = right
        return a_r * a_l, a_r * b_l + b_r

    _, states = lax.associative_scan(combine, (a, bu), axis=1)
    c_mat = lax.complex(c_re.astype(f32), c_im.astype(f32))
    y = jnp.einsum('bsgp,ghp->bsgh', states, c_mat).real + d_skip.astype(f32) * uf
    y = jax.nn.gelu(y.reshape(bsz, seq, D_SSM))
    y = y * jax.nn.sigmoid(y @ w_glu.astype(f32) + b_glu.astype(f32))
    return y.astype(u.dtype)


def chunked_attention(q, k, v, rel_bias):
    bsz, seq, _ = q.shape
    n_chunks = seq // CHUNK
    shp = (bsz, n_chunks, CHUNK, N_HEADS, HEAD_DIM)
    qc, kc, vc = q.reshape(shp), k.reshape(shp), v.reshape(shp)
    pad = ((0, 0), (LEFT_CHUNKS, 0), (0, 0), (0, 0), (0, 0))
    kp, vp = jnp.pad(kc, pad), jnp.pad(vc, pad)
    kb = jnp.concatenate([kp[:, j:j + n_chunks] for j in range(LEFT_CHUNKS + 1)], axis=2)
    vb = jnp.concatenate([vp[:, j:j + n_chunks] for j in range(LEFT_CHUNKS + 1)], axis=2)
    key_chunk = jnp.arange(n_chunks)[:, None] - LEFT_CHUNKS + (jnp.arange(BAND) // CHUNK)[None, :]
    valid = key_chunk >= 0
    rel = jnp.arange(BAND)[None, :] - LEFT_CHUNKS * CHUNK - jnp.arange(CHUNK)[:, None]
    idx = jnp.clip(rel, -MAX_REL, MAX_REL) + MAX_REL
    bias = rel_bias.astype(jnp.float32)[:, idx]
    scores = jnp.einsum('bnqhd,bnkhd->bhnqk', qc, kb).astype(jnp.float32) / math.sqrt(HEAD_DIM)
    scores = scores + bias[None, :, None, :, :]
    scores = jnp.where(valid[None, None, :, None, :], scores, -1e30)
    probs = jax.nn.softmax(scores, axis=-1)
    out = jnp.einsum('bhnqk,bnkhd->bnqhd', probs.astype(v.dtype), vb)
    return out.reshape(bsz, seq, D_ATT)


def setup_inputs(seed: int = 0) -> dict:
    key = jax.random.key(seed)
    ks = jax.random.split(key, 24)
    f32 = jnp.float32
    L, G, P, H = DEPTH, N_SSM_GROUPS, SSM_STATE, SSM_GROUP

    def nrm(k, shape, scale):
        return jax.random.normal(k, shape, f32) * scale

    def gain(k, n):
        return 1.0 + 0.05 * jax.random.normal(k, (L, n), f32)

    lam_re = -0.5 + 0.01 * jax.random.normal(ks[3], (L, G, P), f32)
    lam_im = math.pi * jnp.arange(P, dtype=f32)[None, None, :] + 0.01 * jax.random.normal(ks[4], (L, G, P), f32)
    log_dt = jax.random.uniform(ks[5], (L, G), f32, math.log(DT_MIN), math.log(DT_MAX))
    return {
        "x": jax.random.normal(ks[0], (BATCH, SEQ, D_MODEL), f32),
        "norm_mix_g": gain(ks[1], D_MODEL),
        "w_in": nrm(ks[2], (L, D_MODEL, D_IN), D_MODEL ** -0.5),
        "ssm_lam_re": lam_re,
        "ssm_lam_im": lam_im,
        "ssm_log_dt": log_dt,
        "ssm_b_re": nrm(ks[6], (L, G, P, H), (2.0 * H) ** -0.5),
        "ssm_b_im": nrm(ks[7], (L, G, P, H), (2.0 * H) ** -0.5),
        "ssm_c_re": nrm(ks[8], (L, G, H, P), (2.0 * P) ** -0.5),
        "ssm_c_im": nrm(ks[9], (L, G, H, P), (2.0 * P) ** -0.5),
        "ssm_d": nrm(ks[10], (L, G, H), 1.0),
        "ssm_w_glu": nrm(ks[11], (L, D_SSM, D_SSM), D_SSM ** -0.5),
        "ssm_b_glu": nrm(ks[12], (L, D_SSM), 0.02),
        "attn_rel_bias": nrm(ks[13], (L, N_HEADS, 2 * MAX_REL + 1), 0.5),
        "norm_ssm_out_g": gain(ks[14], D_SSM),
        "norm_att_out_g": gain(ks[15], D_ATT),
        "w_out": nrm(ks[16], (L, D_MIX, D_MODEL), D_MIX ** -0.5),
        "norm_ffn_g": gain(ks[17], D_MODEL),
        "w_gate": nrm(ks[18], (L, D_MODEL, D_FF), D_MODEL ** -0.5),
        "w_up": nrm(ks[19], (L, D_MODEL, D_FF), D_MODEL ** -0.5),
        "w_down": nrm(ks[20], (L, D_FF, D_MODEL), D_FF ** -0.5),
        "norm_final_g": 1.0 + 0.05 * jax.random.normal(ks[21], (D_MODEL,), f32),
    }


def reference(x, norm_mix_g, w_in, ssm_lam_re, ssm_lam_im, ssm_log_dt, ssm_b_re, ssm_b_im,
              ssm_c_re, ssm_c_im, ssm_d, ssm_w_glu, ssm_b_glu, attn_rel_bias,
              norm_ssm_out_g, norm_att_out_g, w_out, norm_ffn_g, w_gate, w_up, w_down,
              norm_final_g):
    for l in range(DEPTH):
        h = rmsnorm(x, norm_mix_g[l])
        proj = h @ w_in[l]
        u_ssm = proj[..., :D_SSM]
        q = proj[..., D_SSM:D_SSM + D_ATT]
        k = proj[..., D_SSM + D_ATT:D_SSM + 2 * D_ATT]
        v = proj[..., D_SSM + 2 * D_ATT:]
        y_ssm = s5_mixer(u_ssm, ssm_lam_re[l], ssm_lam_im[l], ssm_log_dt[l], ssm_b_re[l], ssm_b_im[l],
                         ssm_c_re[l], ssm_c_im[l], ssm_d[l], ssm_w_glu[l], ssm_b_glu[l])
        y_att = chunked_attention(q, k, v, attn_rel_bias[l])
        mixed = jnp.concatenate([rmsnorm(y_ssm, norm_ssm_out_g[l]),
                                 rmsnorm(y_att, norm_att_out_g[l])], axis=-1)
        x = x + mixed @ w_out[l]
        h2 = rmsnorm(x, norm_ffn_g[l])
        x = x + (jax.nn.silu(h2 @ w_gate[l]) * (h2 @ w_up[l])) @ w_down[l]
    return rmsnorm(x, norm_final_g)
```

```python
import functools
import math

import jax
import jax.numpy as jnp
from jax import lax
from jax.experimental import pallas as pl
from jax.experimental.pallas import tpu as pltpu

F32 = jnp.float32
BF16 = jnp.bfloat16

D_MODEL = 1024
D_SSM = 512
D_ATT = 512
SSM_GROUP = 16
N_GROUPS = D_SSM // SSM_GROUP
SSM_STATE = 64
HEAD_DIM = 64
N_HEADS = D_ATT // HEAD_DIM
ATT_CHUNK = 64
LEFT_CHUNKS = 8
MAX_REL = 128
D_FF = 2816
EPS = 1e-6
MASK_VALUE = -1e30

SSM_CHUNK = 64
TOEP_TILE = 256
TOEP_TILES = SSM_CHUNK * SSM_GROUP // TOEP_TILE
Q_BLOCK = 256
KEY_BLOCKS = 3
TB_IN = 4
TB_TAIL = 2
VMEM_LIMIT = 56 * 1024 * 1024


def _sigmoid(z):
    return 1.0 / (1.0 + jnp.exp(-z))


def _inproj_kernel(x_ref, g_ref, wqkv_ref, wut_ref, q_ref, k_ref, v_ref, ut_ref, h_scr, *, nb, nc):
    lanes = nb * nc
    for tl in range(TB_IN):
        xt = x_ref[:, :, tl * D_MODEL:(tl + 1) * D_MODEL].reshape(lanes, D_MODEL)
        ms = jnp.mean(xt * xt, axis=-1, keepdims=True)
        h = xt * lax.rsqrt(ms + EPS) * g_ref[...]
        h_scr[tl * lanes:(tl + 1) * lanes, :] = h.astype(BF16)
    h = h_scr[...]
    qkv = jnp.dot(h, wqkv_ref[...], preferred_element_type=F32)
    ut = lax.dot_general(wut_ref[...], h, (((1,), (1,)), ((), ())),
                         preferred_element_type=F32)
    for tl in range(TB_IN):
        rows = slice(tl * lanes, (tl + 1) * lanes)
        cols = slice(tl * D_ATT, (tl + 1) * D_ATT)
        q_ref[:, :, cols] = qkv[rows, 0:D_ATT].astype(BF16).reshape(nb, nc, D_ATT)
        k_ref[:, :, cols] = qkv[rows, D_ATT:2 * D_ATT].astype(BF16).reshape(nb, nc, D_ATT)
        v_ref[:, :, cols] = qkv[rows, 2 * D_ATT:3 * D_ATT].astype(BF16).reshape(nb, nc, D_ATT)
        ut_ref[:, tl * SSM_GROUP:(tl + 1) * SSM_GROUP, :] = (
            ut[:, rows].reshape(N_GROUPS, SSM_GROUP, lanes).astype(BF16))


def _inproj(x3, g, wqkv, wut, nb, nc):
    lanes = nb * nc
    n_steps = SSM_CHUNK // TB_IN
    const = lambda i: (0, 0)
    att_shape = jax.ShapeDtypeStruct((nb, nc, SSM_CHUNK * D_ATT), BF16)
    att_spec = pl.BlockSpec((nb, nc, TB_IN * D_ATT), lambda i: (0, 0, i))
    return pl.pallas_call(
        functools.partial(_inproj_kernel, nb=nb, nc=nc),
        grid=(n_steps,),
        in_specs=[
            pl.BlockSpec((nb, nc, TB_IN * D_MODEL), lambda i: (0, 0, i)),
            pl.BlockSpec((1, D_MODEL), const),
            pl.BlockSpec((D_MODEL, 3 * D_ATT), const),
            pl.BlockSpec((D_SSM, D_MODEL), const),
        ],
        out_specs=[att_spec, att_spec, att_spec,
                   pl.BlockSpec((N_GROUPS, TB_IN * SSM_GROUP, lanes), lambda i: (0, i, 0))],
        out_shape=[att_shape, att_shape, att_shape,
                   jax.ShapeDtypeStruct((N_GROUPS, SSM_CHUNK * SSM_GROUP, lanes), BF16)],
        scratch_shapes=[pltpu.VMEM((TB_IN * lanes, D_MODEL), BF16)],
        compiler_params=pltpu.CompilerParams(dimension_semantics=("arbitrary",),
                                             vmem_limit_bytes=VMEM_LIMIT),
        name="inproj",
    )(x3, g, wqkv, wut)


def _ssm_kernel(ut_ref, tcol_ref, wst_ref, cpow_ref, are_ref, aim_ref, yg_ref, *, nc):
    u = ut_ref[0]
    lanes = u.shape[1]
    v = jnp.dot(wst_ref[0], u, preferred_element_type=F32)
    xr, xi = v[:SSM_STATE], v[SSM_STATE:]
    chunk = lax.broadcasted_iota(jnp.int32, (SSM_STATE, lanes), 1) % nc
    n_steps = max(1, (nc - 1).bit_length())
    for k in range(n_steps):
        sh = 1 << k
        ar, ai = are_ref[0, k], aim_ref[0, k]
        keep = chunk >= sh
        sr = jnp.where(keep, pltpu.roll(xr, sh, 1), 0.0)
        si = jnp.where(keep, pltpu.roll(xi, sh, 1), 0.0)
        xr, xi = xr + (ar * sr - ai * si), xi + (ar * si + ai * sr)
    keep = chunk >= 1
    pr = jnp.where(keep, pltpu.roll(xr, 1, 1), 0.0)
    pi = jnp.where(keep, pltpu.roll(xi, 1, 1), 0.0)
    xprev = jnp.concatenate([pr, pi], axis=0).astype(BF16)
    y_state = jnp.dot(cpow_ref[0], xprev, preferred_element_type=F32)
    for bi in range(TOEP_TILES):
        acc = y_state[bi * TOEP_TILE:(bi + 1) * TOEP_TILE]
        for bj in range(bi + 1):
            d = bi - bj
            acc = acc + jnp.dot(tcol_ref[0, d * TOEP_TILE:(d + 1) * TOEP_TILE, :],
                                u[bj * TOEP_TILE:(bj + 1) * TOEP_TILE, :],
                                preferred_element_type=F32)
        yg_ref[0, bi * TOEP_TILE:(bi + 1) * TOEP_TILE, :] = jax.nn.gelu(acc).astype(BF16)


def _ssm_core(ut, tcol, wst, cpow, are, aim, nc):
    lanes = ut.shape[2]
    rows = SSM_CHUNK * SSM_GROUP
    n_steps = are.shape[1]
    g3 = lambda g: (g, 0, 0)
    g4 = lambda g: (g, 0, 0, 0)
    return pl.pallas_call(
        functools.partial(_ssm_kernel, nc=nc),
        grid=(N_GROUPS,),
        in_specs=[
            pl.BlockSpec((1, rows, lanes), g3),
            pl.BlockSpec((1, rows, TOEP_TILE), g3),
            pl.BlockSpec((1, 2 * SSM_STATE, rows), g3),
            pl.BlockSpec((1, rows, 2 * SSM_STATE), g3),
            pl.BlockSpec((1, n_steps, SSM_STATE, 1), g4),
            pl.BlockSpec((1, n_steps, SSM_STATE, 1), g4),
        ],
        out_specs=pl.BlockSpec((1, rows, lanes), g3),
        out_shape=jax.ShapeDtypeStruct((N_GROUPS, rows, lanes), BF16),
        compiler_params=pltpu.CompilerParams(dimension_semantics=("arbitrary",),
                                             vmem_limit_bytes=VMEM_LIMIT),
        name="ssm_core",
    )(ut, tcol, wst, cpow, are, aim)


def _attn_kernel(q_ref, k0_ref, k1_ref, k2_ref, v0_ref, v1_ref, v2_ref, tab_ref, g_ref, o_ref, ot_scr):
    i = pl.program_id(1)
    kv_all = ((k0_ref, v0_ref, 0), (k1_ref, v1_ref, 1), (k2_ref, v2_ref, 2))

    def body(kv_list):
        q = q_ref[0]
        vts = [v_ref[0].astype(F32).T.astype(BF16) for (_, v_ref, _) in kv_list]
        lane = lax.broadcasted_iota(jnp.int32, (Q_BLOCK, 2 * HEAD_DIM), 1)
        for hp in range(N_HEADS // 2):
            pair = slice(hp * 2 * HEAD_DIM, (hp + 1) * 2 * HEAD_DIM)
            qp = q[:, pair]
            for e in range(2):
                h = 2 * hp + e
                qh = jnp.where((lane < HEAD_DIM) if e == 0 else (lane >= HEAD_DIM), qp, jnp.zeros_like(qp))
                ss = []
                for (k_ref, _, rel) in kv_list:
                    s = lax.dot_general(k_ref[0, :, pair], qh, (((1,), (1,)), ((), ())),
                                        preferred_element_type=F32)
                    ss.append(s + tab_ref[h, rel * Q_BLOCK:(rel + 1) * Q_BLOCK, :])
                m = ss[0].max(axis=0, keepdims=True)
                for s in ss[1:]:
                    m = jnp.maximum(m, s.max(axis=0, keepdims=True))
                denom = None
                o = None
                for s, vt in zip(ss, vts):
                    p = jnp.exp(s - m)
                    ps = p.sum(axis=0, keepdims=True)
                    denom = ps if denom is None else denom + ps
                    oj = jnp.dot(vt[h * HEAD_DIM:(h + 1) * HEAD_DIM, :], p.astype(BF16),
                                 preferred_element_type=F32)
                    o = oj if o is None else o + oj
                ot_scr[h * HEAD_DIM:(h + 1) * HEAD_DIM, :] = o / denom
        ot = ot_scr[...]
        ms = jnp.mean(ot * ot, axis=0, keepdims=True)
        yn = ot * lax.rsqrt(ms + EPS) * g_ref[...]
        o_ref[0] = yn.T.astype(BF16)

    pl.when(i == 0)(lambda: body(kv_all[2:]))
    pl.when(i == 1)(lambda: body(kv_all[1:]))
    pl.when(i >= 2)(lambda: body(kv_all))


def _attention(q, k, v, table, g_col):
    nb, seq, _ = q.shape
    n_blocks = seq // Q_BLOCK
    blk = (1, Q_BLOCK, D_ATT)

    def back(n):
        return lambda b, i: (b, jnp.maximum(i - n, 0), 0)

    return pl.pallas_call(
        _attn_kernel,
        grid=(nb, n_blocks),
        in_specs=[
            pl.BlockSpec(blk, back(0)),
            pl.BlockSpec(blk, back(2)), pl.BlockSpec(blk, back(1)), pl.BlockSpec(blk, back(0)),
            pl.BlockSpec(blk, back(2)), pl.BlockSpec(blk, back(1)), pl.BlockSpec(blk, back(0)),
            pl.BlockSpec((N_HEADS, KEY_BLOCKS * Q_BLOCK, Q_BLOCK), lambda b, i: (0, 0, 0)),
            pl.BlockSpec((D_ATT, 1), lambda b, i: (0, 0)),
        ],
        out_specs=pl.BlockSpec(blk, back(0)),
        out_shape=jax.ShapeDtypeStruct((nb, seq, D_ATT), BF16),
        scratch_shapes=[pltpu.VMEM((D_ATT, Q_BLOCK), F32)],
        compiler_params=pltpu.CompilerParams(dimension_semantics=("arbitrary", "arbitrary"),
                                             vmem_limit_bytes=VMEM_LIMIT),
        name="chunk_attn",
    )(q, k, k, k, v, v, v, table, g_col)


def _tail_kernel(yg_ref, att_ref, x_ref, wglut_ref, bglu_ref, gssm_ref, wo_ref, gffn_ref,
                 wg_ref, wu_ref, wd_ref, gfin_ref, o_ref, mix_scr, x1_scr, *, nb, nc):
    lanes = nb * nc
    for tl in range(TB_TAIL):
        rows = slice(tl * lanes, (tl + 1) * lanes)
        yg = yg_ref[:, tl * SSM_GROUP:(tl + 1) * SSM_GROUP, :].reshape(D_SSM, lanes)
        z = jnp.dot(wglut_ref[...], yg, preferred_element_type=F32) + bglu_ref[...]
        y2 = yg.astype(F32) * _sigmoid(z)
        ms = jnp.mean(y2 * y2, axis=0, keepdims=True)
        yn = y2 * lax.rsqrt(ms + EPS) * gssm_ref[...]
        mix_scr[rows, 0:D_SSM] = yn.T.astype(BF16)
        mix_scr[rows, D_SSM:D_SSM + D_ATT] = (
            att_ref[:, :, tl * D_ATT:(tl + 1) * D_ATT].reshape(lanes, D_ATT))
    mo = jnp.dot(mix_scr[...], wo_ref[...], preferred_element_type=F32)
    for tl in range(TB_TAIL):
        rows = slice(tl * lanes, (tl + 1) * lanes)
        x1_scr[rows, :] = x_ref[:, :, tl * D_MODEL:(tl + 1) * D_MODEL].reshape(lanes, D_MODEL) + mo[rows]
    x1 = x1_scr[...]
    ms = jnp.mean(x1 * x1, axis=-1, keepdims=True)
    h2 = (x1 * lax.rsqrt(ms + EPS) * gffn_ref[...]).astype(BF16)
    gate = jnp.dot(h2, wg_ref[...], preferred_element_type=F32)
    up = jnp.dot(h2, wu_ref[...], preferred_element_type=F32)
    act = (gate * _sigmoid(gate) * up).astype(BF16)
    x2 = x1 + jnp.dot(act, wd_ref[...], preferred_element_type=F32)
    ms = jnp.mean(x2 * x2, axis=-1, keepdims=True)
    out = x2 * lax.rsqrt(ms + EPS) * gfin_ref[...]
    for tl in range(TB_TAIL):
        rows = slice(tl * lanes, (tl + 1) * lanes)
        o_ref[:, :, tl * D_MODEL:(tl + 1) * D_MODEL] = out[rows].reshape(nb, nc, D_MODEL)


def _tail(yg, att3, x3, wglut, bglu, gssm, wo, gffn, wg, wu, wd, gfin, nb, nc):
    lanes = nb * nc
    n_steps = SSM_CHUNK // TB_TAIL
    const = lambda i: (0, 0)

    def resident(shape):
        return pl.BlockSpec(shape, const, pipeline_mode=pl.Buffered(1))

    return pl.pallas_call(
        functools.partial(_tail_kernel, nb=nb, nc=nc),
        grid=(n_steps,),
        in_specs=[
            pl.BlockSpec((N_GROUPS, TB_TAIL * SSM_GROUP, lanes), lambda i: (0, i, 0)),
            pl.BlockSpec((nb, nc, TB_TAIL * D_ATT), lambda i: (0, 0, i)),
            pl.BlockSpec((nb, nc, TB_TAIL * D_MODEL), lambda i: (0, 0, i)),
            resident((D_SSM, D_SSM)),
            resident((D_SSM, 1)),
            resident((D_SSM, 1)),
            resident((D_MODEL, D_MODEL)),
            resident((1, D_MODEL)),
            resident((D_MODEL, D_FF)),
            resident((D_MODEL, D_FF)),
            resident((D_FF, D_MODEL)),
            resident((1, D_MODEL)),
        ],
        out_specs=pl.BlockSpec((nb, nc, TB_TAIL * D_MODEL), lambda i: (0, 0, i)),
        out_shape=jax.ShapeDtypeStruct((nb, nc, SSM_CHUNK * D_MODEL), F32),
        scratch_shapes=[pltpu.VMEM((TB_TAIL * lanes, D_MODEL), BF16),
                        pltpu.VMEM((TB_TAIL * lanes, D_MODEL), F32)],
        compiler_params=pltpu.CompilerParams(dimension_semantics=("arbitrary",),
                                             vmem_limit_bytes=VMEM_LIMIT),
        name="tail",
    )(yg, att3, x3, wglut, bglu, gssm, wo, gffn, wg, wu, wd, gfin)


def _ssm_params(lam_re, lam_im, log_dt, b_re, b_im, c_re, c_im, d_skip, nc):
    hi = lax.Precision.HIGHEST
    t_len, n_h, n_p = SSM_CHUNK, SSM_GROUP, SSM_STATE
    lam = lax.complex(lam_re, lam_im)
    ldt = lam * jnp.exp(log_dt)[:, None]
    lam_bar = jnp.exp(ldt)
    b_bar = ((lam_bar - 1.0) / lam)[..., None] * lax.complex(b_re, b_im)
    c_mat = lax.complex(c_re, c_im)
    steps = jnp.arange(t_len + 1, dtype=F32)
    pw = jnp.exp(ldt[:, None, :] * steps[None, :, None])
    kern = jnp.einsum('ghp,gjp,gpk->gjhk', c_mat, pw[:, :t_len], b_bar, precision=hi).real
    kern = kern.at[:, 0].add(jax.vmap(jnp.diag)(d_skip))
    t_out = jnp.arange(t_len)[:, None]
    t_in = jnp.arange(TOEP_TILE // n_h)[None, :]
    lag = t_out - t_in
    tcol = jnp.where((lag >= 0)[None, :, :, None, None], kern[:, jnp.clip(lag, 0, t_len - 1)], 0.0)
    tcol = tcol.transpose(0, 1, 3, 2, 4).reshape(N_GROUPS, t_len * n_h, TOEP_TILE)
    wst = pw[:, t_len - 1::-1][:, :t_len].transpose(0, 2, 1)[..., None] * b_bar[:, :, None, :]
    wst = wst.reshape(N_GROUPS, n_p, t_len * n_h)
    wst = jnp.concatenate([wst.real, wst.imag], axis=1)
    cp = c_mat[:, None, :, :] * pw[:, 1:t_len + 1, None, :]
    cp = cp.reshape(N_GROUPS, t_len * n_h, n_p)
    cpow = jnp.concatenate([cp.real, -cp.imag], axis=2)
    n_steps = max(1, (nc - 1).bit_length())
    hops = (t_len * (2.0 ** jnp.arange(n_steps, dtype=F32)))
    apow = jnp.exp(ldt[:, None, :] * hops[None, :, None])[..., None]
    return tcol.astype(BF16), wst.astype(BF16), cpow.astype(BF16), apow.real, apow.imag


def _bias_table(rel_bias):
    kk = jnp.arange(KEY_BLOCKS * Q_BLOCK)[:, None]
    qq = jnp.arange(Q_BLOCK)[None, :]
    rel = kk - (KEY_BLOCKS - 1) * Q_BLOCK - qq
    idx = jnp.clip(rel, -MAX_REL, MAX_REL) + MAX_REL
    kc = kk // ATT_CHUNK - (KEY_BLOCKS - 1) * (Q_BLOCK // ATT_CHUNK)
    qc = qq // ATT_CHUNK
    valid = (kc <= qc) & (kc >= qc - LEFT_CHUNKS)
    return jnp.where(valid[None], rel_bias.astype(F32)[:, idx], MASK_VALUE)


def kernel(x, norm_mix_g, w_in, ssm_lam_re, ssm_lam_im, ssm_log_dt, ssm_b_re, ssm_b_im, ssm_c_re,
           ssm_c_im, ssm_d, ssm_w_glu, ssm_b_glu, attn_rel_bias, norm_ssm_out_g, norm_att_out_g,
           w_out, norm_ffn_g, w_gate, w_up, w_down, norm_final_g):
    nb, seq, _ = x.shape
    nc = seq // SSM_CHUNK
    assert w_in.shape[0] == 1 and seq % Q_BLOCK == 0
    w_in0 = w_in[0]
    scale = 1.0 / math.sqrt(HEAD_DIM)
    wqkv = jnp.concatenate([w_in0[:, D_SSM:D_SSM + D_ATT] * scale, w_in0[:, D_SSM + D_ATT:]],
                           axis=1).astype(BF16)
    wut = w_in0[:, :D_SSM].T.astype(BF16)
    tcol, wst, cpow, are, aim = _ssm_params(ssm_lam_re[0], ssm_lam_im[0], ssm_log_dt[0], ssm_b_re[0],
                                            ssm_b_im[0], ssm_c_re[0], ssm_c_im[0], ssm_d[0], nc)
    table = _bias_table(attn_rel_bias[0])

    x3 = x.reshape(nb, nc, SSM_CHUNK * D_MODEL)
    q3, k3, v3, ut = _inproj(x3, norm_mix_g[0][None, :], wqkv, wut, nb, nc)
    yg = _ssm_core(ut, tcol, wst, cpow, are, aim, nc)
    att = _attention(q3.reshape(nb, seq, D_ATT), k3.reshape(nb, seq, D_ATT), v3.reshape(nb, seq, D_ATT),
                     table, norm_att_out_g[0][:, None])
    out3 = _tail(yg, att.reshape(nb, nc, SSM_CHUNK * D_ATT), x3,
                 ssm_w_glu[0].T.astype(BF16), ssm_b_glu[0][:, None], norm_ssm_out_g[0][:, None],
                 w_out[0].astype(BF16), norm_ffn_g[0][None, :],
                 w_gate[0].astype(BF16), w_up[0].astype(BF16), w_down[0].astype(BF16),
                 norm_final_g[None, :], nb, nc)
    return out3.reshape(nb, seq, D_MODEL)
```

```python
import functools
import math

import jax
import jax.numpy as jnp
from jax import lax
from jax.experimental import pallas as pl
from jax.experimental.pallas import tpu as pltpu

F32 = jnp.float32
BF16 = jnp.bfloat16
HIGHEST = lax.Precision.HIGHEST

D_MODEL = 1024
D_SSM = 512
D_ATT = 512
SSM_GROUP = 16
N_GROUPS = D_SSM // SSM_GROUP
SSM_STATE = 64
HEAD_DIM = 64
N_HEADS = D_ATT // HEAD_DIM
ATT_CHUNK = 64
LEFT_CHUNKS = 8
MAX_REL = 128
D_FF = 2816
EPS = 1e-6
MASK_VALUE = -1e30

SSM_CHUNK = 64
SSM_ROWS = SSM_CHUNK * SSM_GROUP
TOEP_TILE = 256
TOEP_T = TOEP_TILE // SSM_GROUP
TOEP_TILES = SSM_ROWS // TOEP_TILE
LAG_STEPS = SSM_CHUNK + TOEP_T
LAG_COLS = LAG_STEPS * SSM_GROUP
POW_ROWS = 72
Q_BLOCK = 256
KEY_BLOCKS = 3
ROW_TILE = 1024
TB = 8
FFN_ROWS = 512
FF_SPLIT = (0, 1536, D_FF)
IN_SLOTS = 2
TAIL_SLOTS = 3
VMEM_LIMIT = 56 * 1024 * 1024


def _sigmoid(z):
    return 1.0 / (1.0 + jnp.exp(-z))


def _rms_rows(x, g_row):
    ms = jnp.mean(x * x, axis=-1, keepdims=True)
    return x * lax.rsqrt(ms + EPS) * g_row


def _dot(a, b, precision=None):
    return jnp.dot(a, b, preferred_element_type=F32, precision=precision)


def _dot_nt(a, b, precision=None):
    return lax.dot_general(a, b, (((1,), (1,)), ((), ())), preferred_element_type=F32, precision=precision)


def _strided_copies(hbm, buf, sem, step, slot, n_tblocks, to_hbm=False):
    b, i = step // n_tblocks, step % n_tblocks
    copies = []
    for tl in range(TB):
        src, dst = hbm.at[b, :, i * TB + tl, :], buf.at[slot, tl]
        if to_hbm:
            src, dst = dst, src
        copies.append(pltpu.make_async_copy(src, dst, sem.at[slot]))
    return copies


def _qkv_kernel(x_ref, g_ref, w_ref, q_ref, k_ref, v_ref):
    h = _rms_rows(x_ref[...], g_ref[...]).astype(BF16)
    qkv = _dot(h, w_ref[...])
    q_ref[...] = qkv[:, 0:D_ATT].astype(BF16)
    k_ref[...] = qkv[:, D_ATT:2 * D_ATT].astype(BF16)
    v_ref[...] = qkv[:, 2 * D_ATT:3 * D_ATT].astype(BF16)


def _qkv(x2, g_row, wqkv):
    n_rows = x2.shape[0]
    const = lambda i: (0, 0)
    out_shape = jax.ShapeDtypeStruct((n_rows, D_ATT), BF16)
    out_spec = pl.BlockSpec((ROW_TILE, D_ATT), lambda i: (i, 0))
    return pl.pallas_call(
        _qkv_kernel,
        grid=(n_rows // ROW_TILE,),
        in_specs=[pl.BlockSpec((ROW_TILE, D_MODEL), lambda i: (i, 0)),
                  pl.BlockSpec((1, D_MODEL), const),
                  pl.BlockSpec((D_MODEL, 3 * D_ATT), const)],
        out_specs=[out_spec, out_spec, out_spec],
        out_shape=[out_shape, out_shape, out_shape],
        compiler_params=pltpu.CompilerParams(dimension_semantics=("arbitrary",),
                                             vmem_limit_bytes=VMEM_LIMIT),
        name="qkv",
    )(x2, g_row, wqkv)


def _ssm_in_kernel(x_hbm, g_ref, wut_ref, ut_ref, xbuf, h_scr, sem, *, nc, n_tblocks, n_steps):
    step = pl.program_id(0) * n_tblocks + pl.program_id(1)
    slot = step % IN_SLOTS

    @pl.when(step == 0)
    def _():
        for cp in _strided_copies(x_hbm, xbuf, sem, step, slot, n_tblocks):
            cp.start()

    @pl.when(step + 1 < n_steps)
    def _():
        for cp in _strided_copies(x_hbm, xbuf, sem, step + 1, (step + 1) % IN_SLOTS, n_tblocks):
            cp.start()

    for cp in _strided_copies(x_hbm, xbuf, sem, step, slot, n_tblocks):
        cp.wait()
    for tl in range(TB):
        h_scr[tl * nc:(tl + 1) * nc, :] = _rms_rows(xbuf[slot, tl], g_ref[...]).astype(BF16)
    ut = _dot_nt(wut_ref[...], h_scr[...])
    for tl in range(TB):
        ut_ref[:, tl * SSM_GROUP:(tl + 1) * SSM_GROUP, :] = (
            ut[:, tl * nc:(tl + 1) * nc].reshape(N_GROUPS, SSM_GROUP, nc).astype(BF16))


def _ssm_in(x4, g_row, wut):
    nb, nc = x4.shape[0], x4.shape[1]
    n_tblocks = SSM_CHUNK // TB
    return pl.pallas_call(
        functools.partial(_ssm_in_kernel, nc=nc, n_tblocks=n_tblocks, n_steps=nb * n_tblocks),
        grid=(nb, n_tblocks),
        in_specs=[pl.BlockSpec(memory_space=pl.ANY),
                  pl.BlockSpec((1, D_MODEL), lambda b, i: (0, 0)),
                  pl.BlockSpec((D_SSM, D_MODEL), lambda b, i: (0, 0))],
        out_specs=pl.BlockSpec((N_GROUPS, TB * SSM_GROUP, nc), lambda b, i: (0, i, b)),
        out_shape=jax.ShapeDtypeStruct((N_GROUPS, SSM_ROWS, nb * nc), BF16),
        scratch_shapes=[pltpu.VMEM((IN_SLOTS, TB, nc, D_MODEL), F32),
                        pltpu.VMEM((TB * nc, D_MODEL), BF16),
                        pltpu.SemaphoreType.DMA((IN_SLOTS,))],
        compiler_params=pltpu.CompilerParams(dimension_semantics=("arbitrary", "arbitrary"),
                                             vmem_limit_bytes=VMEM_LIMIT),
        name="ssm_in",
    )(x4, g_row, wut)


def _cexp(re, im):
    mag = jnp.exp(re)
    return mag * jnp.cos(im), mag * jnp.sin(im)


def _ssm_kernel(ut_ref, lam_ref, ldt_ref, bre_ref, bim_ref, cre_ref, cim_ref, d_ref,
                yg_ref, mt_scr, cpow_scr, tcol_scr, *, nc, n_steps):
    n_p, n_h = SSM_STATE, SSM_GROUP
    dt = jnp.exp(ldt_ref[0])
    lam_r, lam_i = lam_ref[0, 0:1, :], lam_ref[0, 1:2, :]
    ar, ai = lam_r * dt, lam_i * dt
    lbr, lbi = _cexp(ar, ai)
    den = lam_r * lam_r + lam_i * lam_i
    fr = ((lbr - 1.0) * lam_r + lbi * lam_i) / den
    fi = (lbi * lam_r - (lbr - 1.0) * lam_i) / den
    b_re, b_im = bre_ref[0], bim_ref[0]
    bbr, bbi = fr * b_re - fi * b_im, fr * b_im + fi * b_re
    c_re, c_im = cre_ref[0], cim_ref[0]
    jrow = lax.broadcasted_iota(jnp.int32, (POW_ROWS, 1), 0).astype(F32)
    pwr, pwi = _cexp(ar * jrow, ai * jrow)
    re_half = lax.broadcasted_iota(jnp.int32, (n_h, 2 * n_p), 1) < n_p
    for m in range(SSM_CHUNK):
        j = SSM_CHUNK - 1 - m
        pr, pi = pwr[j:j + 1, :], pwi[j:j + 1, :]
        mt_scr[m * n_h:(m + 1) * n_h, :] = jnp.where(re_half, bbr * pr - bbi * pi, bbr * pi + bbi * pr)
    mt_scr[SSM_ROWS:, :] = jnp.zeros((LAG_COLS - SSM_ROWS, 2 * n_p), F32)
    for t in range(SSM_CHUNK):
        pr, pi = pwr[t + 1:t + 2, :], pwi[t + 1:t + 2, :]
        cpow_scr[t * n_h:(t + 1) * n_h, :] = jnp.where(
            re_half, c_re * pr - c_im * pi, -(c_re * pi + c_im * pr)).astype(BF16)
    mt = mt_scr[...]
    lagk = _dot_nt(jnp.where(re_half, c_re, -c_im), mt, HIGHEST)
    col_h = lax.broadcasted_iota(jnp.int32, (n_h, LAG_COLS), 1)
    row_h = lax.broadcasted_iota(jnp.int32, (n_h, LAG_COLS), 0)
    lagk = lagk + jnp.where(col_h % n_h == row_h, d_ref[0], 0.0)
    for t_out in range(SSM_CHUNK):
        start = (SSM_CHUNK - 1 - t_out) * n_h
        tcol_scr[t_out * n_h:(t_out + 1) * n_h, :] = lagk[:, start:start + TOEP_TILE].astype(BF16)
    u = ut_ref[0]
    lanes = u.shape[1]
    v = _dot(mt[:SSM_ROWS, :].T.astype(BF16), u)
    xr, xi = v[:n_p], v[n_p:]
    eye = (lax.broadcasted_iota(jnp.int32, (n_p, 2 * n_p), 0)
           == lax.broadcasted_iota(jnp.int32, (n_p, 2 * n_p), 1))
    a_r = jnp.sum(jnp.where(eye, pwr[SSM_CHUNK:SSM_CHUNK + 1, :], 0.0), axis=1, keepdims=True)
    a_i = jnp.sum(jnp.where(eye, pwi[SSM_CHUNK:SSM_CHUNK + 1, :], 0.0), axis=1, keepdims=True)
    chunk = lax.broadcasted_iota(jnp.int32, (n_p, lanes), 1) % nc
    for k in range(n_steps):
        sh = 1 << k
        keep = chunk >= sh
        sr = jnp.where(keep, pltpu.roll(xr, sh, 1), 0.0)
        si = jnp.where(keep, pltpu.roll(xi, sh, 1), 0.0)
        xr, xi = xr + (a_r * sr - a_i * si), xi + (a_r * si + a_i * sr)
        a_r, a_i = a_r * a_r - a_i * a_i, 2.0 * a_r * a_i
    keep = chunk >= 1
    pr = jnp.where(keep, pltpu.roll(xr, 1, 1), 0.0)
    pi = jnp.where(keep, pltpu.roll(xi, 1, 1), 0.0)
    xprev = jnp.concatenate([pr, pi], axis=0).astype(BF16)
    for bi in range(TOEP_TILES):
        rows = slice(bi * TOEP_TILE, (bi + 1) * TOEP_TILE)
        acc = _dot(cpow_scr[rows, :], xprev)
        for bj in range(bi + 1):
            d = bi - bj
            acc = acc + _dot(tcol_scr[d * TOEP_TILE:(d + 1) * TOEP_TILE, :],
                             u[bj * TOEP_TILE:(bj + 1) * TOEP_TILE, :])
        yg_ref[0, rows, :] = jax.nn.gelu(acc).astype(BF16)


def _ssm_core(ut, lam2, log_dt, b_re, b_im, c_re, c_im, d_pad, nc):
    lanes = ut.shape[2]
    n_steps = max(1, (nc - 1).bit_length())
    g3 = lambda g: (g, 0, 0)
    par = pl.BlockSpec((1, SSM_GROUP, 2 * SSM_STATE), g3)
    return pl.pallas_call(
        functools.partial(_ssm_kernel, nc=nc, n_steps=n_steps),
        grid=(N_GROUPS,),
        in_specs=[
            pl.BlockSpec((1, SSM_ROWS, lanes), g3),
            pl.BlockSpec((1, 2, 2 * SSM_STATE), g3),
            pl.BlockSpec((1, 1, 1), g3),
            par, par, par, par,
            pl.BlockSpec((1, 1, LAG_COLS), g3),
        ],
        out_specs=pl.BlockSpec((1, SSM_ROWS, lanes), g3),
        out_shape=jax.ShapeDtypeStruct((N_GROUPS, SSM_ROWS, lanes), BF16),
        scratch_shapes=[pltpu.VMEM((LAG_COLS, 2 * SSM_STATE), F32),
                        pltpu.VMEM((SSM_ROWS, 2 * SSM_STATE), BF16),
                        pltpu.VMEM((SSM_ROWS, TOEP_TILE), BF16)],
        compiler_params=pltpu.CompilerParams(dimension_semantics=("arbitrary",),
                                             vmem_limit_bytes=VMEM_LIMIT),
        name="ssm_core",
    )(ut, lam2, log_dt, b_re, b_im, c_re, c_im, d_pad)


def _attn_kernel(q_ref, k0_ref, k1_ref, k2_ref, v0_ref, v1_ref, v2_ref, tab_ref, g_ref, x_ref, wo_ref,
                 o_ref, ot_scr):
    i = pl.program_id(1)
    kv_all = ((k0_ref, v0_ref, 0), (k1_ref, v1_ref, 1), (k2_ref, v2_ref, 2))

    def body(kv_list):
        q = q_ref[0]
        vts = [v_ref[0].astype(F32).T.astype(BF16) for (_, v_ref, _) in kv_list]
        lane = lax.broadcasted_iota(jnp.int32, (Q_BLOCK, 2 * HEAD_DIM), 1)
        for hp in range(N_HEADS // 2):
            pair = slice(hp * 2 * HEAD_DIM, (hp + 1) * 2 * HEAD_DIM)
            qp = q[:, pair]
            for e in range(2):
                h = 2 * hp + e
                qh = jnp.where((lane < HEAD_DIM) if e == 0 else (lane >= HEAD_DIM), qp, jnp.zeros_like(qp))
                ss = []
                for (k_ref, _, rel) in kv_list:
                    s = _dot_nt(k_ref[0, :, pair], qh)
                    ss.append(s + tab_ref[h, rel * Q_BLOCK:(rel + 1) * Q_BLOCK, :])
                m = ss[0].max(axis=0, keepdims=True)
                for s in ss[1:]:
                    m = jnp.maximum(m, s.max(axis=0, keepdims=True))
                denom = None
                o = None
                for s, vt in zip(ss, vts):
                    p = jnp.exp(s - m)
                    ps = p.sum(axis=0, keepdims=True)
                    denom = ps if denom is None else denom + ps
                    oj = _dot(vt[h * HEAD_DIM:(h + 1) * HEAD_DIM, :], p.astype(BF16))
                    o = oj if o is None else o + oj
                ot_scr[h * HEAD_DIM:(h + 1) * HEAD_DIM, :] = o / denom
        ot = ot_scr[...]
        ms = jnp.mean(ot * ot, axis=0, keepdims=True)
        yn = (ot * lax.rsqrt(ms + EPS) * g_ref[...]).T.astype(BF16)
        o_ref[0] = x_ref[0] + _dot(yn, wo_ref[...])

    pl.when(i == 0)(lambda: body(kv_all[2:]))
    pl.when(i == 1)(lambda: body(kv_all[1:]))
    pl.when(i >= 2)(lambda: body(kv_all))


def _attention(q, k, v, table, g_col, x, wo_att):
    nb, seq, _ = q.shape
    blk = (1, Q_BLOCK, D_ATT)
    xblk = (1, Q_BLOCK, D_MODEL)

    def back(n):
        return lambda b, i: (b, jnp.maximum(i - n, 0), 0)

    return pl.pallas_call(
        _attn_kernel,
        grid=(nb, seq // Q_BLOCK),
        in_specs=[
            pl.BlockSpec(blk, back(0)),
            pl.BlockSpec(blk, back(2)), pl.BlockSpec(blk, back(1)), pl.BlockSpec(blk, back(0)),
            pl.BlockSpec(blk, back(2)), pl.BlockSpec(blk, back(1)), pl.BlockSpec(blk, back(0)),
            pl.BlockSpec((N_HEADS, KEY_BLOCKS * Q_BLOCK, Q_BLOCK), lambda b, i: (0, 0, 0)),
            pl.BlockSpec((D_ATT, 1), lambda b, i: (0, 0)),
            pl.BlockSpec(xblk, back(0)),
            pl.BlockSpec((D_ATT, D_MODEL), lambda b, i: (0, 0)),
        ],
        out_specs=pl.BlockSpec(xblk, back(0)),
        out_shape=jax.ShapeDtypeStruct((nb, seq, D_MODEL), F32),
        scratch_shapes=[pltpu.VMEM((D_ATT, Q_BLOCK), F32)],
        compiler_params=pltpu.CompilerParams(dimension_semantics=("arbitrary", "arbitrary"),
                                             vmem_limit_bytes=VMEM_LIMIT),
        name="chunk_attn",
    )(q, k, k, k, v, v, v, table, g_col, x, wo_att)


def _tail_kernel(yg_ref, x_hbm, wglut_ref, bglu_ref, gssm_ref, wo_ref, gffn_ref, wg_ref, wu_ref, wd_ref,
                 gfin_ref, o_hbm, xbuf, mix_scr, in_sem, out_sem, *, nc, n_tblocks, n_steps):
    step = pl.program_id(0) * n_tblocks + pl.program_id(1)
    slot = step % TAIL_SLOTS
    gather = functools.partial(_strided_copies, x_hbm, xbuf, in_sem, n_tblocks=n_tblocks)
    scatter = functools.partial(_strided_copies, o_hbm, xbuf, out_sem, n_tblocks=n_tblocks, to_hbm=True)

    @pl.when(step == 0)
    def _():
        for cp in gather(step, slot):
            cp.start()

    @pl.when(step + 1 < n_steps)
    def _():
        nxt = (step + 1) % TAIL_SLOTS

        @pl.when(step + 1 >= TAIL_SLOTS)
        def _():
            for cp in scatter(step + 1 - TAIL_SLOTS, nxt):
                cp.wait()

        for cp in gather(step + 1, nxt):
            cp.start()

    for tl in range(TB):
        yg = yg_ref[:, tl * SSM_GROUP:(tl + 1) * SSM_GROUP, :].reshape(D_SSM, nc)
        z = _dot(wglut_ref[...], yg) + bglu_ref[...]
        y2 = yg.astype(F32) * _sigmoid(z)
        ms = jnp.mean(y2 * y2, axis=0, keepdims=True)
        yn = y2 * lax.rsqrt(ms + EPS) * gssm_ref[...]
        mix_scr[tl * nc:(tl + 1) * nc, :] = yn.T.astype(BF16)

    for cp in gather(step, slot):
        cp.wait()
    t_per = FFN_ROWS // nc
    for sb in range(TB // t_per):
        rows = slice(sb * FFN_ROWS, (sb + 1) * FFN_ROWS)
        x_rows = jnp.concatenate([xbuf[slot, sb * t_per + j] for j in range(t_per)], axis=0)
        x1 = x_rows + _dot(mix_scr[rows, :], wo_ref[...])
        h2 = _rms_rows(x1, gffn_ref[...]).astype(BF16)
        x2 = x1
        for lo, hi in zip(FF_SPLIT[:-1], FF_SPLIT[1:]):
            gate = _dot(h2, wg_ref[:, lo:hi])
            up = _dot(h2, wu_ref[:, lo:hi])
            act = (gate * _sigmoid(gate) * up).astype(BF16)
            x2 = x2 + _dot(act, wd_ref[lo:hi, :])
        out = _rms_rows(x2, gfin_ref[...])
        for j in range(t_per):
            xbuf[slot, sb * t_per + j] = out[j * nc:(j + 1) * nc]

    for cp in scatter(step, slot):
        cp.start()

    @pl.when(step == n_steps - 1)
    def _():
        for back in range(min(TAIL_SLOTS, n_steps)):
            for cp in scatter(step - back, (step - back) % TAIL_SLOTS):
                cp.wait()


def _tail(yg, x4, wglut, bglu, gssm, wo_ssm, gffn, wg, wu, wd, gfin):
    nb, nc = x4.shape[0], x4.shape[1]
    n_tblocks = SSM_CHUNK // TB
    const = lambda b, i: (0, 0)

    def resident(shape):
        return pl.BlockSpec(shape, const, pipeline_mode=pl.Buffered(1))

    return pl.pallas_call(
        functools.partial(_tail_kernel, nc=nc, n_tblocks=n_tblocks, n_steps=nb * n_tblocks),
        grid=(nb, n_tblocks),
        in_specs=[
            pl.BlockSpec((N_GROUPS, TB * SSM_GROUP, nc), lambda b, i: (0, i, b)),
            pl.BlockSpec(memory_space=pl.ANY),
            resident((D_SSM, D_SSM)),
            resident((D_SSM, 1)),
            resident((D_SSM, 1)),
            resident((D_SSM, D_MODEL)),
            resident((1, D_MODEL)),
            resident((D_MODEL, D_FF)),
            resident((D_MODEL, D_FF)),
            resident((D_FF, D_MODEL)),
            resident((1, D_MODEL)),
        ],
        out_specs=pl.BlockSpec(memory_space=pl.ANY),
        out_shape=jax.ShapeDtypeStruct(x4.shape, F32),
        scratch_shapes=[pltpu.VMEM((TAIL_SLOTS, TB, nc, D_MODEL), F32),
                        pltpu.VMEM((TB * nc, D_SSM), BF16),
                        pltpu.SemaphoreType.DMA((TAIL_SLOTS,)),
                        pltpu.SemaphoreType.DMA((TAIL_SLOTS,))],
        compiler_params=pltpu.CompilerParams(dimension_semantics=("arbitrary", "arbitrary"),
                                             vmem_limit_bytes=VMEM_LIMIT),
        name="tail",
    )(yg, x4, wglut, bglu, gssm, wo_ssm, gffn, wg, wu, wd, gfin)


def _bias_table(rel_bias):
    kk = jnp.arange(KEY_BLOCKS * Q_BLOCK)[:, None]
    qq = jnp.arange(Q_BLOCK)[None, :]
    rel = kk - (KEY_BLOCKS - 1) * Q_BLOCK - qq
    idx = jnp.clip(rel, -MAX_REL, MAX_REL) + MAX_REL
    kc = kk // ATT_CHUNK - (KEY_BLOCKS - 1) * (Q_BLOCK // ATT_CHUNK)
    qc = qq // ATT_CHUNK
    valid = (kc <= qc) & (kc >= qc - LEFT_CHUNKS)
    return jnp.where(valid[None], rel_bias.astype(F32)[:, idx], MASK_VALUE)


def _twice(a):
    return jnp.concatenate([a, a], axis=-1)


def kernel(x, norm_mix_g, w_in, ssm_lam_re, ssm_lam_im, ssm_log_dt, ssm_b_re, ssm_b_im, ssm_c_re,
           ssm_c_im, ssm_d, ssm_w_glu, ssm_b_glu, attn_rel_bias, norm_ssm_out_g, norm_att_out_g,
           w_out, norm_ffn_g, w_gate, w_up, w_down, norm_final_g):
    nb, seq, _ = x.shape
    nc = seq // SSM_CHUNK
    assert w_in.shape[0] == 1 and seq % Q_BLOCK == 0 and (nb * seq) % ROW_TILE == 0
    assert FFN_ROWS % nc == 0 and (TB * nc) % FFN_ROWS == 0
    w_in0 = w_in[0]
    scale = 1.0 / math.sqrt(HEAD_DIM)
    wqkv = jnp.concatenate([w_in0[:, D_SSM:D_SSM + D_ATT] * scale, w_in0[:, D_SSM + D_ATT:]],
                           axis=1).astype(BF16)
    wut = w_in0[:, :D_SSM].T.astype(BF16)
    wo = w_out[0].astype(BF16)
    g_mix = norm_mix_g[0][None, :]
    x4 = x.reshape(nb, nc, SSM_CHUNK, D_MODEL)
    lam2 = _twice(jnp.stack([ssm_lam_re[0], ssm_lam_im[0]], axis=1))
    b_re2, b_im2 = _twice(ssm_b_re[0].transpose(0, 2, 1)), _twice(ssm_b_im[0].transpose(0, 2, 1))
    c_re2, c_im2 = _twice(ssm_c_re[0]), _twice(ssm_c_im[0])
    lag0 = (SSM_CHUNK - 1) * SSM_GROUP
    d_pad = jnp.pad(ssm_d[0], ((0, 0), (lag0, LAG_COLS - lag0 - SSM_GROUP)))[:, None, :]

    q, k, v = _qkv(x.reshape(nb * seq, D_MODEL), g_mix, wqkv)
    ut = _ssm_in(x4, g_mix, wut)
    yg = _ssm_core(ut, lam2, ssm_log_dt[0][:, None, None], b_re2, b_im2, c_re2, c_im2, d_pad, nc)
    x1a = _attention(q.reshape(nb, seq, D_ATT), k.reshape(nb, seq, D_ATT), v.reshape(nb, seq, D_ATT),
                     _bias_table(attn_rel_bias[0]), norm_att_out_g[0][:, None], x, wo[D_SSM:])
    out4 = _tail(yg, x1a.reshape(nb, nc, SSM_CHUNK, D_MODEL),
                 ssm_w_glu[0].T.astype(BF16), ssm_b_glu[0][:, None], norm_ssm_out_g[0][:, None],
                 wo[:D_SSM], norm_ffn_g[0][None, :],
                 w_gate[0].astype(BF16), w_up[0].astype(BF16), w_down[0].astype(BF16),
                 norm_final_g[None, :])
    return out4.reshape(nb, seq, D_MODEL)
```

```python
import functools
import math

import jax
import jax.numpy as jnp
from jax import lax
from jax.experimental import pallas as pl
from jax.experimental.pallas import tpu as pltpu

F32 = jnp.float32
BF16 = jnp.bfloat16
HIGHEST = lax.Precision.HIGHEST

D_MODEL = 1024
D_SSM = 512
D_ATT = 512
SSM_GROUP = 16
N_GROUPS = D_SSM // SSM_GROUP
SSM_STATE = 64
HEAD_DIM = 64
N_HEADS = D_ATT // HEAD_DIM
ATT_CHUNK = 64
LEFT_CHUNKS = 8
MAX_REL = 128
D_FF = 2816
EPS = 1e-6
MASK_VALUE = -1e30

SSM_CHUNK = 64
SSM_ROWS = SSM_CHUNK * SSM_GROUP
TOEP_TILE = 256
TOEP_T = TOEP_TILE // SSM_GROUP
TOEP_TILES = SSM_ROWS // TOEP_TILE
LAG_STEPS = SSM_CHUNK + TOEP_T
LAG_COLS = LAG_STEPS * SSM_GROUP
POW_ROWS = 72
Q_BLOCK = 256
KEY_BLOCKS = 3
REL_WIDTH = (KEY_BLOCKS + 1) * Q_BLOCK
REL_PAD = 384
ROW_TILE = 1024
TB = 8
FFN_ROWS = 512
FF_SPLIT = (0, 1536, D_FF)
IN_SLOTS = 2
TAIL_SLOTS = 3
VMEM_LIMIT = 56 * 1024 * 1024


def _sigmoid(z):
    return 1.0 / (1.0 + jnp.exp(-z))


def _rms_rows(x, g_row):
    ms = jnp.mean(x * x, axis=-1, keepdims=True)
    return x * lax.rsqrt(ms + EPS) * g_row


def _dot(a, b, precision=None):
    return jnp.dot(a, b, preferred_element_type=F32, precision=precision)


def _dot_nt(a, b, precision=None):
    return lax.dot_general(a, b, (((1,), (1,)), ((), ())), preferred_element_type=F32, precision=precision)


def _strided_copies(hbm, buf, sem, step, slot, n_tblocks, to_hbm=False):
    b, i = step // n_tblocks, step % n_tblocks
    copies = []
    for tl in range(TB):
        src, dst = hbm.at[b, :, i * TB + tl, :], buf.at[slot, tl]
        if to_hbm:
            src, dst = dst, src
        copies.append(pltpu.make_async_copy(src, dst, sem.at[slot]))
    return copies


def _qkv_kernel(x_ref, g_ref, w_ref, q_ref, k_ref, v_ref):
    h = _rms_rows(x_ref[...], g_ref[...]).astype(BF16)
    qkv = _dot(h, w_ref[...])
    q_ref[...] = qkv[:, 0:D_ATT].astype(BF16)
    k_ref[...] = qkv[:, D_ATT:2 * D_ATT].astype(BF16)
    v_ref[...] = qkv[:, 2 * D_ATT:3 * D_ATT].astype(BF16)


def _qkv(x2, g_row, wqkv):
    n_rows = x2.shape[0]
    const = lambda i: (0, 0)
    out_shape = jax.ShapeDtypeStruct((n_rows, D_ATT), BF16)
    out_spec = pl.BlockSpec((ROW_TILE, D_ATT), lambda i: (i, 0))
    return pl.pallas_call(
        _qkv_kernel,
        grid=(n_rows // ROW_TILE,),
        in_specs=[pl.BlockSpec((ROW_TILE, D_MODEL), lambda i: (i, 0)),
                  pl.BlockSpec((1, D_MODEL), const),
                  pl.BlockSpec((D_MODEL, 3 * D_ATT), const)],
        out_specs=[out_spec, out_spec, out_spec],
        out_shape=[out_shape, out_shape, out_shape],
        compiler_params=pltpu.CompilerParams(dimension_semantics=("arbitrary",),
                                             vmem_limit_bytes=VMEM_LIMIT),
        name="qkv",
    )(x2, g_row, wqkv)


def _ssm_in_kernel(x_hbm, g_ref, wut_ref, ut_ref, xbuf, h_scr, sem, *, nc, n_tblocks, n_steps):
    step = pl.program_id(0) * n_tblocks + pl.program_id(1)
    slot = step % IN_SLOTS

    @pl.when(step == 0)
    def _():
        for cp in _strided_copies(x_hbm, xbuf, sem, step, slot, n_tblocks):
            cp.start()

    @pl.when(step + 1 < n_steps)
    def _():
        for cp in _strided_copies(x_hbm, xbuf, sem, step + 1, (step + 1) % IN_SLOTS, n_tblocks):
            cp.start()

    for cp in _strided_copies(x_hbm, xbuf, sem, step, slot, n_tblocks):
        cp.wait()
    for tl in range(TB):
        h_scr[tl * nc:(tl + 1) * nc, :] = _rms_rows(xbuf[slot, tl], g_ref[...]).astype(BF16)
    ut = _dot_nt(wut_ref[...], h_scr[...])
    for tl in range(TB):
        ut_ref[:, tl * SSM_GROUP:(tl + 1) * SSM_GROUP, :] = (
            ut[:, tl * nc:(tl + 1) * nc].reshape(N_GROUPS, SSM_GROUP, nc).astype(BF16))


def _ssm_in(x4, g_row, wut):
    nb, nc = x4.shape[0], x4.shape[1]
    n_tblocks = SSM_CHUNK // TB
    return pl.pallas_call(
        functools.partial(_ssm_in_kernel, nc=nc, n_tblocks=n_tblocks, n_steps=nb * n_tblocks),
        grid=(nb, n_tblocks),
        in_specs=[pl.BlockSpec(memory_space=pl.ANY),
                  pl.BlockSpec((1, D_MODEL), lambda b, i: (0, 0)),
                  pl.BlockSpec((D_SSM, D_MODEL), lambda b, i: (0, 0))],
        out_specs=pl.BlockSpec((N_GROUPS, TB * SSM_GROUP, nc), lambda b, i: (0, i, b)),
        out_shape=jax.ShapeDtypeStruct((N_GROUPS, SSM_ROWS, nb * nc), BF16),
        scratch_shapes=[pltpu.VMEM((IN_SLOTS, TB, nc, D_MODEL), F32),
                        pltpu.VMEM((TB * nc, D_MODEL), BF16),
                        pltpu.SemaphoreType.DMA((IN_SLOTS,))],
        compiler_params=pltpu.CompilerParams(dimension_semantics=("arbitrary", "arbitrary"),
                                             vmem_limit_bytes=VMEM_LIMIT),
        name="ssm_in",
    )(x4, g_row, wut)


def _cexp(re, im):
    mag = jnp.exp(re)
    return mag * jnp.cos(im), mag * jnp.sin(im)


def _ssm_kernel(ut_ref, lam_ref, ldt_ref, bre_ref, bim_ref, cre_ref, cim_ref, d_ref,
                yg_ref, mt_scr, cpow_scr, tcol_scr, *, nc, n_steps):
    n_p, n_h = SSM_STATE, SSM_GROUP
    dt = jnp.exp(ldt_ref[0])
    lam_r, lam_i = lam_ref[0, 0:1, :], lam_ref[0, 1:2, :]
    ar, ai = lam_r * dt, lam_i * dt
    lbr, lbi = _cexp(ar, ai)
    den = lam_r * lam_r + lam_i * lam_i
    fr = ((lbr - 1.0) * lam_r + lbi * lam_i) / den
    fi = (lbi * lam_r - (lbr - 1.0) * lam_i) / den
    b_re, b_im = bre_ref[0], bim_ref[0]
    bbr, bbi = fr * b_re - fi * b_im, fr * b_im + fi * b_re
    c_re, c_im = cre_ref[0], cim_ref[0]
    jrow = lax.broadcasted_iota(jnp.int32, (POW_ROWS, 1), 0).astype(F32)
    pwr, pwi = _cexp(ar * jrow, ai * jrow)
    re_half = lax.broadcasted_iota(jnp.int32, (n_h, 2 * n_p), 1) < n_p
    for m in range(SSM_CHUNK):
        j = SSM_CHUNK - 1 - m
        pr, pi = pwr[j:j + 1, :], pwi[j:j + 1, :]
        mt_scr[m * n_h:(m + 1) * n_h, :] = jnp.where(re_half, bbr * pr - bbi * pi, bbr * pi + bbi * pr)
    mt_scr[SSM_ROWS:, :] = jnp.zeros((LAG_COLS - SSM_ROWS, 2 * n_p), F32)
    for t in range(SSM_CHUNK):
        pr, pi = pwr[t + 1:t + 2, :], pwi[t + 1:t + 2, :]
        cpow_scr[t * n_h:(t + 1) * n_h, :] = jnp.where(
            re_half, c_re * pr - c_im * pi, -(c_re * pi + c_im * pr)).astype(BF16)
    mt = mt_scr[...]
    lagk = _dot_nt(jnp.where(re_half, c_re, -c_im), mt, HIGHEST)
    col_h = lax.broadcasted_iota(jnp.int32, (n_h, LAG_COLS), 1)
    row_h = lax.broadcasted_iota(jnp.int32, (n_h, LAG_COLS), 0)
    lagk = lagk + jnp.where(col_h % n_h == row_h, d_ref[0], 0.0)
    for t_out in range(SSM_CHUNK):
        start = (SSM_CHUNK - 1 - t_out) * n_h
        tcol_scr[t_out * n_h:(t_out + 1) * n_h, :] = lagk[:, start:start + TOEP_TILE].astype(BF16)
    u = ut_ref[0]
    lanes = u.shape[1]
    v = _dot(mt[:SSM_ROWS, :].T.astype(BF16), u)
    xr, xi = v[:n_p], v[n_p:]
    eye = (lax.broadcasted_iota(jnp.int32, (n_p, 2 * n_p), 0)
           == lax.broadcasted_iota(jnp.int32, (n_p, 2 * n_p), 1))
    a_r = jnp.sum(jnp.where(eye, pwr[SSM_CHUNK:SSM_CHUNK + 1, :], 0.0), axis=1, keepdims=True)
    a_i = jnp.sum(jnp.where(eye, pwi[SSM_CHUNK:SSM_CHUNK + 1, :], 0.0), axis=1, keepdims=True)
    chunk = lax.broadcasted_iota(jnp.int32, (n_p, lanes), 1) % nc
    for k in range(n_steps):
        sh = 1 << k
        keep = chunk >= sh
        sr = jnp.where(keep, pltpu.roll(xr, sh, 1), 0.0)
        si = jnp.where(keep, pltpu.roll(xi, sh, 1), 0.0)
        xr, xi = xr + (a_r * sr - a_i * si), xi + (a_r * si + a_i * sr)
        a_r, a_i = a_r * a_r - a_i * a_i, 2.0 * a_r * a_i
    keep = chunk >= 1
    pr = jnp.where(keep, pltpu.roll(xr, 1, 1), 0.0)
    pi = jnp.where(keep, pltpu.roll(xi, 1, 1), 0.0)
    xprev = jnp.concatenate([pr, pi], axis=0).astype(BF16)
    for bi in range(TOEP_TILES):
        rows = slice(bi * TOEP_TILE, (bi + 1) * TOEP_TILE)
        acc = _dot(cpow_scr[rows, :], xprev)
        for bj in range(bi + 1):
            d = bi - bj
            acc = acc + _dot(tcol_scr[d * TOEP_TILE:(d + 1) * TOEP_TILE, :],
                             u[bj * TOEP_TILE:(bj + 1) * TOEP_TILE, :])
        yg_ref[0, rows, :] = jax.nn.gelu(acc).astype(BF16)


def _ssm_core(ut, lam2, log_dt, b_re, b_im, c_re, c_im, d_pad, nc):
    lanes = ut.shape[2]
    n_steps = max(1, (nc - 1).bit_length())
    g3 = lambda g: (g, 0, 0)
    par = pl.BlockSpec((1, SSM_GROUP, 2 * SSM_STATE), g3)
    return pl.pallas_call(
        functools.partial(_ssm_kernel, nc=nc, n_steps=n_steps),
        grid=(N_GROUPS,),
        in_specs=[
            pl.BlockSpec((1, SSM_ROWS, lanes), g3),
            pl.BlockSpec((1, 2, 2 * SSM_STATE), g3),
            pl.BlockSpec((1, 1, 1), g3),
            par, par, par, par,
            pl.BlockSpec((1, 1, LAG_COLS), g3),
        ],
        out_specs=pl.BlockSpec((1, SSM_ROWS, lanes), g3),
        out_shape=jax.ShapeDtypeStruct((N_GROUPS, SSM_ROWS, lanes), BF16),
        scratch_shapes=[pltpu.VMEM((LAG_COLS, 2 * SSM_STATE), F32),
                        pltpu.VMEM((SSM_ROWS, 2 * SSM_STATE), BF16),
                        pltpu.VMEM((SSM_ROWS, TOEP_TILE), BF16)],
        compiler_params=pltpu.CompilerParams(dimension_semantics=("arbitrary",),
                                             vmem_limit_bytes=VMEM_LIMIT),
        name="ssm_core",
    )(ut, lam2, log_dt, b_re, b_im, c_re, c_im, d_pad)


def _build_bias_table(rb_ref, tab_scr):
    n_keys, width = KEY_BLOCKS * Q_BLOCK, REL_WIDTH
    s = lax.broadcasted_iota(jnp.int32, (1, width), 1)
    delta = jnp.where(s < Q_BLOCK, -s, width - s)
    rel = delta - (KEY_BLOCKS - 1) * Q_BLOCK
    idx = jnp.clip(rel, -MAX_REL, MAX_REL) + MAX_REL
    pick = (lax.broadcasted_iota(jnp.int32, (REL_PAD, width), 0) == idx).astype(F32)
    w = _dot(rb_ref[...], pick, HIGHEST)
    kk = lax.broadcasted_iota(jnp.int32, (n_keys, Q_BLOCK), 0)
    qq = lax.broadcasted_iota(jnp.int32, (n_keys, Q_BLOCK), 1)
    kc = kk // ATT_CHUNK - (KEY_BLOCKS - 1) * (Q_BLOCK // ATT_CHUNK)
    qc = qq // ATT_CHUNK
    valid = (kc <= qc) & (kc >= qc - LEFT_CHUNKS)
    for h in range(N_HEADS):
        rows = jnp.broadcast_to(w[h:h + 1, :], (n_keys, width))
        toep = pltpu.roll(rows, 0, 1, stride=1, stride_axis=0)
        tab_scr[h] = jnp.where(valid, toep[:, :Q_BLOCK], MASK_VALUE)


def _attn_kernel(q_ref, k0_ref, k1_ref, k2_ref, v0_ref, v1_ref, v2_ref, rb_ref, g_ref, x_ref, wo_ref,
                 o_ref, tab_scr, ot_scr):
    i = pl.program_id(1)
    kv_all = ((k0_ref, v0_ref, 0), (k1_ref, v1_ref, 1), (k2_ref, v2_ref, 2))

    @pl.when((pl.program_id(0) == 0) & (i == 0))
    def _():
        _build_bias_table(rb_ref, tab_scr)

    def body(kv_list):
        q = q_ref[0]
        vts = [v_ref[0].astype(F32).T.astype(BF16) for (_, v_ref, _) in kv_list]
        lane = lax.broadcasted_iota(jnp.int32, (Q_BLOCK, 2 * HEAD_DIM), 1)

        def scores(h):
            pair = slice((h // 2) * 2 * HEAD_DIM, (h // 2 + 1) * 2 * HEAD_DIM)
            qp = q[:, pair]
            qh = jnp.where((lane < HEAD_DIM) if h % 2 == 0 else (lane >= HEAD_DIM), qp, jnp.zeros_like(qp))
            return [_dot_nt(k_ref[0, :, pair], qh) + tab_scr[h, rel * Q_BLOCK:(rel + 1) * Q_BLOCK, :]
                    for (k_ref, _, rel) in kv_list]

        ss_next = scores(0)
        for h in range(N_HEADS):
            ss = ss_next
            if h + 1 < N_HEADS:
                ss_next = scores(h + 1)
            m = ss[0].max(axis=0, keepdims=True)
            for s in ss[1:]:
                m = jnp.maximum(m, s.max(axis=0, keepdims=True))
            denom = None
            o = None
            for s, vt in zip(ss, vts):
                p = jnp.exp(s - m)
                ps = p.sum(axis=0, keepdims=True)
                denom = ps if denom is None else denom + ps
                oj = _dot(vt[h * HEAD_DIM:(h + 1) * HEAD_DIM, :], p.astype(BF16))
                o = oj if o is None else o + oj
            ot_scr[h * HEAD_DIM:(h + 1) * HEAD_DIM, :] = o / denom
        ot = ot_scr[...]
        ms = jnp.mean(ot * ot, axis=0, keepdims=True)
        yn = (ot * lax.rsqrt(ms + EPS) * g_ref[...]).T.astype(BF16)
        o_ref[0] = x_ref[0] + _dot(yn, wo_ref[...])

    pl.when(i == 0)(lambda: body(kv_all[2:]))
    pl.when(i == 1)(lambda: body(kv_all[1:]))
    pl.when(i >= 2)(lambda: body(kv_all))


def _attention(q, k, v, rb_pad, g_col, x, wo_att):
    nb, seq, _ = q.shape
    blk = (1, Q_BLOCK, D_ATT)
    xblk = (1, Q_BLOCK, D_MODEL)

    def back(n):
        return lambda b, i: (b, jnp.maximum(i - n, 0), 0)

    return pl.pallas_call(
        _attn_kernel,
        grid=(nb, seq // Q_BLOCK),
        in_specs=[
            pl.BlockSpec(blk, back(0)),
            pl.BlockSpec(blk, back(2)), pl.BlockSpec(blk, back(1)), pl.BlockSpec(blk, back(0)),
            pl.BlockSpec(blk, back(2)), pl.BlockSpec(blk, back(1)), pl.BlockSpec(blk, back(0)),
            pl.BlockSpec((N_HEADS, REL_PAD), lambda b, i: (0, 0)),
            pl.BlockSpec((D_ATT, 1), lambda b, i: (0, 0)),
            pl.BlockSpec(xblk, back(0)),
            pl.BlockSpec((D_ATT, D_MODEL), lambda b, i: (0, 0)),
        ],
        out_specs=pl.BlockSpec(xblk, back(0)),
        out_shape=jax.ShapeDtypeStruct((nb, seq, D_MODEL), F32),
        scratch_shapes=[pltpu.VMEM((N_HEADS, KEY_BLOCKS * Q_BLOCK, Q_BLOCK), F32),
                        pltpu.VMEM((D_ATT, Q_BLOCK), F32)],
        compiler_params=pltpu.CompilerParams(dimension_semantics=("arbitrary", "arbitrary"),
                                             vmem_limit_bytes=VMEM_LIMIT),
        name="chunk_attn",
    )(q, k, k, k, v, v, v, rb_pad, g_col, x, wo_att)


def _tail_kernel(yg_ref, x_hbm, wglut_ref, bglu_ref, gssm_ref, wo_ref, gffn_ref, wg_ref, wu_ref, wd_ref,
                 gfin_ref, o_hbm, xbuf, mix_scr, in_sem, out_sem, *, nc, n_tblocks, n_steps):
    step = pl.program_id(0) * n_tblocks + pl.program_id(1)
    slot = step % TAIL_SLOTS
    gather = functools.partial(_strided_copies, x_hbm, xbuf, in_sem, n_tblocks=n_tblocks)
    scatter = functools.partial(_strided_copies, o_hbm, xbuf, out_sem, n_tblocks=n_tblocks, to_hbm=True)

    @pl.when(step == 0)
    def _():
        for cp in gather(step, slot):
            cp.start()

    @pl.when(step + 1 < n_steps)
    def _():
        nxt = (step + 1) % TAIL_SLOTS

        @pl.when(step + 1 >= TAIL_SLOTS)
        def _():
            for cp in scatter(step + 1 - TAIL_SLOTS, nxt):
                cp.wait()

        for cp in gather(step + 1, nxt):
            cp.start()

    for tl in range(TB):
        yg = yg_ref[:, tl * SSM_GROUP:(tl + 1) * SSM_GROUP, :].reshape(D_SSM, nc)
        z = _dot(wglut_ref[...], yg) + bglu_ref[...]
        y2 = yg.astype(F32) * _sigmoid(z)
        ms = jnp.mean(y2 * y2, axis=0, keepdims=True)
        yn = y2 * lax.rsqrt(ms + EPS) * gssm_ref[...]
        mix_scr[tl * nc:(tl + 1) * nc, :] = yn.T.astype(BF16)

    for cp in gather(step, slot):
        cp.wait()
    t_per = FFN_ROWS // nc
    for sb in range(TB // t_per):
        rows = slice(sb * FFN_ROWS, (sb + 1) * FFN_ROWS)
        x_rows = jnp.concatenate([xbuf[slot, sb * t_per + j] for j in range(t_per)], axis=0)
        x1 = x_rows + _dot(mix_scr[rows, :], wo_ref[...])
        h2 = _rms_rows(x1, gffn_ref[...]).astype(BF16)
        x2 = x1
        for lo, hi in zip(FF_SPLIT[:-1], FF_SPLIT[1:]):
            gate = _dot(h2, wg_ref[:, lo:hi])
            up = _dot(h2, wu_ref[:, lo:hi])
            act = (gate * _sigmoid(gate) * up).astype(BF16)
            x2 = x2 + _dot(act, wd_ref[lo:hi, :])
        out = _rms_rows(x2, gfin_ref[...])
        for j in range(t_per):
            xbuf[slot, sb * t_per + j] = out[j * nc:(j + 1) * nc]

    for cp in scatter(step, slot):
        cp.start()

    @pl.when(step == n_steps - 1)
    def _():
        for back in range(min(TAIL_SLOTS, n_steps)):
            for cp in scatter(step - back, (step - back) % TAIL_SLOTS):
                cp.wait()


def _tail(yg, x4, wglut, bglu, gssm, wo_ssm, gffn, wg, wu, wd, gfin):
    nb, nc = x4.shape[0], x4.shape[1]
    n_tblocks = SSM_CHUNK // TB
    const = lambda b, i: (0, 0)

    def resident(shape):
        return pl.BlockSpec(shape, const, pipeline_mode=pl.Buffered(1))

    return pl.pallas_call(
        functools.partial(_tail_kernel, nc=nc, n_tblocks=n_tblocks, n_steps=nb * n_tblocks),
        grid=(nb, n_tblocks),
        in_specs=[
            pl.BlockSpec((N_GROUPS, TB * SSM_GROUP, nc), lambda b, i: (0, i, b)),
            pl.BlockSpec(memory_space=pl.ANY),
            resident((D_SSM, D_SSM)),
            resident((D_SSM, 1)),
            resident((D_SSM, 1)),
            resident((D_SSM, D_MODEL)),
            resident((1, D_MODEL)),
            resident((D_MODEL, D_FF)),
            resident((D_MODEL, D_FF)),
            resident((D_FF, D_MODEL)),
            resident((1, D_MODEL)),
        ],
        out_specs=pl.BlockSpec(memory_space=pl.ANY),
        out_shape=jax.ShapeDtypeStruct(x4.shape, F32),
        scratch_shapes=[pltpu.VMEM((TAIL_SLOTS, TB, nc, D_MODEL), F32),
                        pltpu.VMEM((TB * nc, D_SSM), BF16),
                        pltpu.SemaphoreType.DMA((TAIL_SLOTS,)),
                        pltpu.SemaphoreType.DMA((TAIL_SLOTS,))],
        compiler_params=pltpu.CompilerParams(dimension_semantics=("arbitrary", "arbitrary"),
                                             vmem_limit_bytes=VMEM_LIMIT),
        name="tail",
    )(yg, x4, wglut, bglu, gssm, wo_ssm, gffn, wg, wu, wd, gfin)


def _twice(a):
    return jnp.concatenate([a, a], axis=-1)


def kernel(x, norm_mix_g, w_in, ssm_lam_re, ssm_lam_im, ssm_log_dt, ssm_b_re, ssm_b_im, ssm_c_re,
           ssm_c_im, ssm_d, ssm_w_glu, ssm_b_glu, attn_rel_bias, norm_ssm_out_g, norm_att_out_g,
           w_out, norm_ffn_g, w_gate, w_up, w_down, norm_final_g):
    nb, seq, _ = x.shape
    nc = seq // SSM_CHUNK
    assert w_in.shape[0] == 1 and seq % Q_BLOCK == 0 and (nb * seq) % ROW_TILE == 0
    assert FFN_ROWS % nc == 0 and (TB * nc) % FFN_ROWS == 0
    w_in0 = w_in[0]
    scale = 1.0 / math.sqrt(HEAD_DIM)
    wqkv = jnp.concatenate([w_in0[:, D_SSM:D_SSM + D_ATT] * scale, w_in0[:, D_SSM + D_ATT:]],
                           axis=1).astype(BF16)
    wut = w_in0[:, :D_SSM].T.astype(BF16)
    wo = w_out[0].astype(BF16)
    g_mix = norm_mix_g[0][None, :]
    x4 = x.reshape(nb, nc, SSM_CHUNK, D_MODEL)
    lam2 = _twice(jnp.stack([ssm_lam_re[0], ssm_lam_im[0]], axis=1))
    b_re2, b_im2 = _twice(ssm_b_re[0].transpose(0, 2, 1)), _twice(ssm_b_im[0].transpose(0, 2, 1))
    c_re2, c_im2 = _twice(ssm_c_re[0]), _twice(ssm_c_im[0])
    lag0 = (SSM_CHUNK - 1) * SSM_GROUP
    d_pad = jnp.pad(ssm_d[0], ((0, 0), (lag0, LAG_COLS - lag0 - SSM_GROUP)))[:, None, :]

    q, k, v = _qkv(x.reshape(nb * seq, D_MODEL), g_mix, wqkv)
    ut = _ssm_in(x4, g_mix, wut)
    yg = _ssm_core(ut, lam2, ssm_log_dt[0][:, None, None], b_re2, b_im2, c_re2, c_im2, d_pad, nc)
    x1a = _attention(q.reshape(nb, seq, D_ATT), k.reshape(nb, seq, D_ATT), v.reshape(nb, seq, D_ATT),
                     jnp.pad(attn_rel_bias[0], ((0, 0), (0, REL_PAD - (2 * MAX_REL + 1)))),
                     norm_att_out_g[0][:, None], x, wo[D_SSM:])
    out4 = _tail(yg, x1a.reshape(nb, nc, SSM_CHUNK, D_MODEL),
                 ssm_w_glu[0].T.astype(BF16), ssm_b_glu[0][:, None], norm_ssm_out_g[0][:, None],
                 wo[:D_SSM], norm_ffn_g[0][None, :],
                 w_gate[0].astype(BF16), w_up[0].astype(BF16), w_down[0].astype(BF16),
                 norm_final_g[None, :])
    return out4.reshape(nb, seq, D_MODEL)
```

```python
import functools
import math

import jax
import jax.numpy as jnp
from jax import lax
from jax.experimental import pallas as pl
from jax.experimental.pallas import tpu as pltpu

F32 = jnp.float32
BF16 = jnp.bfloat16
HIGHEST = lax.Precision.HIGHEST

D_MODEL = 1024
D_SSM = 512
D_ATT = 512
SSM_GROUP = 16
N_GROUPS = D_SSM // SSM_GROUP
SSM_STATE = 64
HEAD_DIM = 64
N_HEADS = D_ATT // HEAD_DIM
ATT_CHUNK = 64
LEFT_CHUNKS = 8
MAX_REL = 128
D_FF = 2816
EPS = 1e-6
MASK_VALUE = -1e30
LOG2E = math.log2(math.e)

SSM_CHUNK = 64
SSM_ROWS = SSM_CHUNK * SSM_GROUP
TOEP_TILE = 256
TOEP_T = TOEP_TILE // SSM_GROUP
TOEP_TILES = SSM_ROWS // TOEP_TILE
LAG_STEPS = SSM_CHUNK + TOEP_T
LAG_COLS = LAG_STEPS * SSM_GROUP
POW_ROWS = 72
Q_BLOCK = 256
HALF_Q = 128
KEY_BLOCKS = 3
REL_WIDTH = (KEY_BLOCKS + 1) * Q_BLOCK
REL_PAD = 384
ROW_TILE = 1024
TB = 8
FFN_ROWS = 512
FF_SPLIT = (0, 1536, D_FF)
IN_SLOTS = 2
TAIL_SLOTS = 3
VMEM_LIMIT = 56 * 1024 * 1024


def _sigmoid(z):
    return 1.0 / (1.0 + jnp.exp(-z))


def _rms_rows(x, g_row):
    ms = jnp.mean(x * x, axis=-1, keepdims=True)
    return x * lax.rsqrt(ms + EPS) * g_row


def _dot(a, b, precision=None):
    return jnp.dot(a, b, preferred_element_type=F32, precision=precision)


def _dot_nt(a, b, precision=None):
    return lax.dot_general(a, b, (((1,), (1,)), ((), ())), preferred_element_type=F32, precision=precision)


def _strided_copies(hbm, buf, sem, step, slot, n_tblocks, to_hbm=False):
    b, i = step // n_tblocks, step % n_tblocks
    copies = []
    for tl in range(TB):
        src, dst = hbm.at[b, :, i * TB + tl, :], buf.at[slot, tl]
        if to_hbm:
            src, dst = dst, src
        copies.append(pltpu.make_async_copy(src, dst, sem.at[slot]))
    return copies


def _qkv_kernel(x_ref, g_ref, w_ref, q_ref, k_ref, v_ref):
    h = _rms_rows(x_ref[...], g_ref[...]).astype(BF16)
    qkv = _dot(h, w_ref[...])
    q_ref[...] = qkv[:, 0:D_ATT].astype(BF16)
    k_ref[...] = qkv[:, D_ATT:2 * D_ATT].astype(BF16)
    v_ref[...] = qkv[:, 2 * D_ATT:3 * D_ATT].astype(BF16)


def _qkv(x2, g_row, wqkv):
    n_rows = x2.shape[0]
    const = lambda i: (0, 0)
    out_shape = jax.ShapeDtypeStruct((n_rows, D_ATT), BF16)
    out_spec = pl.BlockSpec((ROW_TILE, D_ATT), lambda i: (i, 0))
    return pl.pallas_call(
        _qkv_kernel,
        grid=(n_rows // ROW_TILE,),
        in_specs=[pl.BlockSpec((ROW_TILE, D_MODEL), lambda i: (i, 0)),
                  pl.BlockSpec((1, D_MODEL), const),
                  pl.BlockSpec((D_MODEL, 3 * D_ATT), const)],
        out_specs=[out_spec, out_spec, out_spec],
        out_shape=[out_shape, out_shape, out_shape],
        compiler_params=pltpu.CompilerParams(dimension_semantics=("arbitrary",),
                                             vmem_limit_bytes=VMEM_LIMIT),
        name="qkv",
    )(x2, g_row, wqkv)


def _ssm_in_kernel(x_hbm, g_ref, wut_ref, ut_ref, xbuf, h_scr, sem, *, nc, n_tblocks, n_steps):
    step = pl.program_id(0) * n_tblocks + pl.program_id(1)
    slot = step % IN_SLOTS

    @pl.when(step == 0)
    def _():
        for cp in _strided_copies(x_hbm, xbuf, sem, step, slot, n_tblocks):
            cp.start()

    @pl.when(step + 1 < n_steps)
    def _():
        for cp in _strided_copies(x_hbm, xbuf, sem, step + 1, (step + 1) % IN_SLOTS, n_tblocks):
            cp.start()

    for cp in _strided_copies(x_hbm, xbuf, sem, step, slot, n_tblocks):
        cp.wait()
    for tl in range(TB):
        h_scr[tl * nc:(tl + 1) * nc, :] = _rms_rows(xbuf[slot, tl], g_ref[...]).astype(BF16)
    ut = _dot_nt(wut_ref[...], h_scr[...])
    for tl in range(TB):
        ut_ref[:, tl * SSM_GROUP:(tl + 1) * SSM_GROUP, :] = (
            ut[:, tl * nc:(tl + 1) * nc].reshape(N_GROUPS, SSM_GROUP, nc).astype(BF16))


def _ssm_in(x4, g_row, wut):
    nb, nc = x4.shape[0], x4.shape[1]
    n_tblocks = SSM_CHUNK // TB
    return pl.pallas_call(
        functools.partial(_ssm_in_kernel, nc=nc, n_tblocks=n_tblocks, n_steps=nb * n_tblocks),
        grid=(nb, n_tblocks),
        in_specs=[pl.BlockSpec(memory_space=pl.ANY),
                  pl.BlockSpec((1, D_MODEL), lambda b, i: (0, 0)),
                  pl.BlockSpec((D_SSM, D_MODEL), lambda b, i: (0, 0))],
        out_specs=pl.BlockSpec((N_GROUPS, TB * SSM_GROUP, nc), lambda b, i: (0, i, b)),
        out_shape=jax.ShapeDtypeStruct((N_GROUPS, SSM_ROWS, nb * nc), BF16),
        scratch_shapes=[pltpu.VMEM((IN_SLOTS, TB, nc, D_MODEL), F32),
                        pltpu.VMEM((TB * nc, D_MODEL), BF16),
                        pltpu.SemaphoreType.DMA((IN_SLOTS,))],
        compiler_params=pltpu.CompilerParams(dimension_semantics=("arbitrary", "arbitrary"),
                                             vmem_limit_bytes=VMEM_LIMIT),
        name="ssm_in",
    )(x4, g_row, wut)


def _cexp(re, im):
    mag = jnp.exp(re)
    return mag * jnp.cos(im), mag * jnp.sin(im)


def _ssm_kernel(ut_ref, lam_ref, ldt_ref, bre_ref, bim_ref, cre_ref, cim_ref, d_ref,
                yg_ref, mt_scr, cpow_scr, tcol_scr, *, nc, n_steps):
    n_p, n_h = SSM_STATE, SSM_GROUP
    dt = jnp.exp(ldt_ref[0])
    lam_r, lam_i = lam_ref[0, 0:1, :], lam_ref[0, 1:2, :]
    ar, ai = lam_r * dt, lam_i * dt
    lbr, lbi = _cexp(ar, ai)
    den = lam_r * lam_r + lam_i * lam_i
    fr = ((lbr - 1.0) * lam_r + lbi * lam_i) / den
    fi = (lbi * lam_r - (lbr - 1.0) * lam_i) / den
    b_re, b_im = bre_ref[0], bim_ref[0]
    bbr, bbi = fr * b_re - fi * b_im, fr * b_im + fi * b_re
    c_re, c_im = cre_ref[0], cim_ref[0]
    jrow = lax.broadcasted_iota(jnp.int32, (POW_ROWS, 1), 0).astype(F32)
    pwr, pwi = _cexp(ar * jrow, ai * jrow)
    re_half = lax.broadcasted_iota(jnp.int32, (n_h, 2 * n_p), 1) < n_p
    for m in range(SSM_CHUNK):
        j = SSM_CHUNK - 1 - m
        pr, pi = pwr[j:j + 1, :], pwi[j:j + 1, :]
        mt_scr[m * n_h:(m + 1) * n_h, :] = jnp.where(re_half, bbr * pr - bbi * pi, bbr * pi + bbi * pr)
    mt_scr[SSM_ROWS:, :] = jnp.zeros((LAG_COLS - SSM_ROWS, 2 * n_p), F32)
    for t in range(SSM_CHUNK):
        pr, pi = pwr[t + 1:t + 2, :], pwi[t + 1:t + 2, :]
        cpow_scr[t * n_h:(t + 1) * n_h, :] = jnp.where(
            re_half, c_re * pr - c_im * pi, -(c_re * pi + c_im * pr)).astype(BF16)
    mt = mt_scr[...]
    lagk = _dot_nt(jnp.where(re_half, c_re, -c_im), mt, HIGHEST)
    col_h = lax.broadcasted_iota(jnp.int32, (n_h, LAG_COLS), 1)
    row_h = lax.broadcasted_iota(jnp.int32, (n_h, LAG_COLS), 0)
    lagk = lagk + jnp.where(col_h % n_h == row_h, d_ref[0], 0.0)
    for t_out in range(SSM_CHUNK):
        start = (SSM_CHUNK - 1 - t_out) * n_h
        tcol_scr[t_out * n_h:(t_out + 1) * n_h, :] = lagk[:, start:start + TOEP_TILE].astype(BF16)
    u = ut_ref[0]
    lanes = u.shape[1]
    v = _dot(mt[:SSM_ROWS, :].T.astype(BF16), u)
    xr, xi = v[:n_p], v[n_p:]
    eye = (lax.broadcasted_iota(jnp.int32, (n_p, 2 * n_p), 0)
           == lax.broadcasted_iota(jnp.int32, (n_p, 2 * n_p), 1))
    a_r = jnp.sum(jnp.where(eye, pwr[SSM_CHUNK:SSM_CHUNK + 1, :], 0.0), axis=1, keepdims=True)
    a_i = jnp.sum(jnp.where(eye, pwi[SSM_CHUNK:SSM_CHUNK + 1, :], 0.0), axis=1, keepdims=True)
    chunk = lax.broadcasted_iota(jnp.int32, (n_p, lanes), 1) % nc
    for k in range(n_steps):
        sh = 1 << k
        keep = chunk >= sh
        sr = jnp.where(keep, pltpu.roll(xr, sh, 1), 0.0)
        si = jnp.where(keep, pltpu.roll(xi, sh, 1), 0.0)
        xr, xi = xr + (a_r * sr - a_i * si), xi + (a_r * si + a_i * sr)
        a_r, a_i = a_r * a_r - a_i * a_i, 2.0 * a_r * a_i
    keep = chunk >= 1
    pr = jnp.where(keep, pltpu.roll(xr, 1, 1), 0.0)
    pi = jnp.where(keep, pltpu.roll(xi, 1, 1), 0.0)
    xprev = jnp.concatenate([pr, pi], axis=0).astype(BF16)
    for bi in range(TOEP_TILES):
        rows = slice(bi * TOEP_TILE, (bi + 1) * TOEP_TILE)
        acc = _dot(cpow_scr[rows, :], xprev)
        for bj in range(bi + 1):
            d = bi - bj
            acc = acc + _dot(tcol_scr[d * TOEP_TILE:(d + 1) * TOEP_TILE, :],
                             u[bj * TOEP_TILE:(bj + 1) * TOEP_TILE, :])
        yg_ref[0, rows, :] = jax.nn.gelu(acc).astype(BF16)


def _ssm_core(ut, lam2, log_dt, b_re, b_im, c_re, c_im, d_pad, nc):
    lanes = ut.shape[2]
    n_steps = max(1, (nc - 1).bit_length())
    g3 = lambda g: (g, 0, 0)
    par = pl.BlockSpec((1, SSM_GROUP, 2 * SSM_STATE), g3)
    return pl.pallas_call(
        functools.partial(_ssm_kernel, nc=nc, n_steps=n_steps),
        grid=(N_GROUPS,),
        in_specs=[
            pl.BlockSpec((1, SSM_ROWS, lanes), g3),
            pl.BlockSpec((1, 2, 2 * SSM_STATE), g3),
            pl.BlockSpec((1, 1, 1), g3),
            par, par, par, par,
            pl.BlockSpec((1, 1, LAG_COLS), g3),
        ],
        out_specs=pl.BlockSpec((1, SSM_ROWS, lanes), g3),
        out_shape=jax.ShapeDtypeStruct((N_GROUPS, SSM_ROWS, lanes), BF16),
        scratch_shapes=[pltpu.VMEM((LAG_COLS, 2 * SSM_STATE), F32),
                        pltpu.VMEM((SSM_ROWS, 2 * SSM_STATE), BF16),
                        pltpu.VMEM((SSM_ROWS, TOEP_TILE), BF16)],
        compiler_params=pltpu.CompilerParams(dimension_semantics=("arbitrary",),
                                             vmem_limit_bytes=VMEM_LIMIT),
        name="ssm_core",
    )(ut, lam2, log_dt, b_re, b_im, c_re, c_im, d_pad)


def _build_bias_table(rb_ref, tab_scr):
    n_keys, width = KEY_BLOCKS * Q_BLOCK, REL_WIDTH
    s = lax.broadcasted_iota(jnp.int32, (1, width), 1)
    delta = jnp.where(s < Q_BLOCK, -s, width - s)
    rel = delta - (KEY_BLOCKS - 1) * Q_BLOCK
    idx = jnp.clip(rel, -MAX_REL, MAX_REL) + MAX_REL
    pick = (lax.broadcasted_iota(jnp.int32, (REL_PAD, width), 0) == idx).astype(F32)
    w = _dot(rb_ref[...], pick, HIGHEST)
    kk = lax.broadcasted_iota(jnp.int32, (n_keys, Q_BLOCK), 0)
    qq = lax.broadcasted_iota(jnp.int32, (n_keys, Q_BLOCK), 1)
    kc = kk // ATT_CHUNK - (KEY_BLOCKS - 1) * (Q_BLOCK // ATT_CHUNK)
    qc = qq // ATT_CHUNK
    valid = (kc <= qc) & (kc >= qc - LEFT_CHUNKS)
    for h in range(N_HEADS):
        rows = jnp.broadcast_to(w[h:h + 1, :], (n_keys, width))
        toep = pltpu.roll(rows, 0, 1, stride=1, stride_axis=0)
        tab_scr[h] = jnp.where(valid, toep[:, :Q_BLOCK] * LOG2E, MASK_VALUE)


def _attn_kernel(q_ref, k0_ref, k1_ref, k2_ref, v0_ref, v1_ref, v2_ref, rb_ref, g_ref, x_ref, wo_ref,
                 o_ref, tab_scr, ot_scr):
    i = pl.program_id(1)
    kv_all = ((k0_ref, v0_ref, 0), (k1_ref, v1_ref, 1), (k2_ref, v2_ref, 2))

    @pl.when((pl.program_id(0) == 0) & (i == 0))
    def _():
        _build_bias_table(rb_ref, tab_scr)

    def body(kv_list):
        q = q_ref[0]
        vts = [v_ref[0].astype(F32).T.astype(BF16) for (_, v_ref, _) in kv_list]
        lane = lax.broadcasted_iota(jnp.int32, (Q_BLOCK, 2 * HEAD_DIM), 1)

        def scores(h):
            pair = slice((h // 2) * 2 * HEAD_DIM, (h // 2 + 1) * 2 * HEAD_DIM)
            qp = q[:, pair]
            qh = jnp.where((lane < HEAD_DIM) if h % 2 == 0 else (lane >= HEAD_DIM), qp, jnp.zeros_like(qp))
            return [_dot_nt(k_ref[0, :, pair], qh) for (k_ref, _, _) in kv_list]

        def probs(h, raw, half):
            cols = slice(half * HALF_Q, (half + 1) * HALF_Q)
            band_lo = half * HALF_Q
            band_hi = band_lo + (LEFT_CHUNKS + HALF_Q // ATT_CHUNK) * ATT_CHUNK
            parts = []
            for s, (_, _, rel) in zip(raw, kv_list):
                lo, hi = max(band_lo - rel * Q_BLOCK, 0), min(band_hi - rel * Q_BLOCK, Q_BLOCK)
                parts.append((lo, hi, s[lo:hi, cols] + tab_scr[h, rel * Q_BLOCK + lo:rel * Q_BLOCK + hi, cols]))
            m = parts[0][2].max(axis=0, keepdims=True)
            for _, _, s in parts[1:]:
                m = jnp.maximum(m, s.max(axis=0, keepdims=True))
            denom, out = None, []
            for lo, hi, s in parts:
                p = jnp.exp2(s - m)
                ps = p.sum(axis=0, keepdims=True)
                denom = ps if denom is None else denom + ps
                p = p.astype(BF16)
                pad = [jnp.zeros((n, HALF_Q), BF16) for n in (lo, Q_BLOCK - hi)]
                out.append(jnp.concatenate([z for z in (pad[0], p, pad[1]) if z.shape[0]], axis=0))
            return out, denom

        raw_next = scores(0)
        for h in range(N_HEADS):
            raw = raw_next
            if h + 1 < N_HEADS:
                raw_next = scores(h + 1)
            (p_lo, d_lo), (p_hi, d_hi) = probs(h, raw, 0), probs(h, raw, 1)
            o = None
            for pl_, ph_, vt in zip(p_lo, p_hi, vts):
                oj = _dot(vt[h * HEAD_DIM:(h + 1) * HEAD_DIM, :], jnp.concatenate([pl_, ph_], axis=1))
                o = oj if o is None else o + oj
            ot_scr[h * HEAD_DIM:(h + 1) * HEAD_DIM, :] = o / jnp.concatenate([d_lo, d_hi], axis=1)
        ot = ot_scr[...]
        ms = jnp.mean(ot * ot, axis=0, keepdims=True)
        yn = (ot * lax.rsqrt(ms + EPS) * g_ref[...]).T.astype(BF16)
        o_ref[0] = x_ref[0] + _dot(yn, wo_ref[...])

    pl.when(i == 0)(lambda: body(kv_all[2:]))
    pl.when(i == 1)(lambda: body(kv_all[1:]))
    pl.when(i >= 2)(lambda: body(kv_all))


def _attention(q, k, v, rb_pad, g_col, x, wo_att):
    nb, seq, _ = q.shape
    blk = (1, Q_BLOCK, D_ATT)
    xblk = (1, Q_BLOCK, D_MODEL)

    def back(n):
        return lambda b, i: (b, jnp.maximum(i - n, 0), 0)

    return pl.pallas_call(
        _attn_kernel,
        grid=(nb, seq // Q_BLOCK),
        in_specs=[
            pl.BlockSpec(blk, back(0)),
            pl.BlockSpec(blk, back(2)), pl.BlockSpec(blk, back(1)), pl.BlockSpec(blk, back(0)),
            pl.BlockSpec(blk, back(2)), pl.BlockSpec(blk, back(1)), pl.BlockSpec(blk, back(0)),
            pl.BlockSpec((N_HEADS, REL_PAD), lambda b, i: (0, 0)),
            pl.BlockSpec((D_ATT, 1), lambda b, i: (0, 0)),
            pl.BlockSpec(xblk, back(0)),
            pl.BlockSpec((D_ATT, D_MODEL), lambda b, i: (0, 0)),
        ],
        out_specs=pl.BlockSpec(xblk, back(0)),
        out_shape=jax.ShapeDtypeStruct((nb, seq, D_MODEL), F32),
        scratch_shapes=[pltpu.VMEM((N_HEADS, KEY_BLOCKS * Q_BLOCK, Q_BLOCK), F32),
                        pltpu.VMEM((D_ATT, Q_BLOCK), F32)],
        compiler_params=pltpu.CompilerParams(dimension_semantics=("arbitrary", "arbitrary"),
                                             vmem_limit_bytes=VMEM_LIMIT),
        name="chunk_attn",
    )(q, k, k, k, v, v, v, rb_pad, g_col, x, wo_att)


def _tail_kernel(yg_ref, x_hbm, wglut_ref, bglu_ref, gssm_ref, wo_ref, gffn_ref, wg_ref, wu_ref, wd_ref,
                 gfin_ref, o_hbm, xbuf, mix_scr, in_sem, out_sem, *, nc, n_tblocks, n_steps):
    step = pl.program_id(0) * n_tblocks + pl.program_id(1)
    slot = step % TAIL_SLOTS
    gather = functools.partial(_strided_copies, x_hbm, xbuf, in_sem, n_tblocks=n_tblocks)
    scatter = functools.partial(_strided_copies, o_hbm, xbuf, out_sem, n_tblocks=n_tblocks, to_hbm=True)

    @pl.when(step == 0)
    def _():
        for cp in gather(step, slot):
            cp.start()

    @pl.when(step + 1 < n_steps)
    def _():
        nxt = (step + 1) % TAIL_SLOTS

        @pl.when(step + 1 >= TAIL_SLOTS)
        def _():
            for cp in scatter(step + 1 - TAIL_SLOTS, nxt):
                cp.wait()

        for cp in gather(step + 1, nxt):
            cp.start()

    for tl in range(TB):
        yg = yg_ref[:, tl * SSM_GROUP:(tl + 1) * SSM_GROUP, :].reshape(D_SSM, nc)
        z = _dot(wglut_ref[...], yg) + bglu_ref[...]
        y2 = yg.astype(F32) * _sigmoid(z)
        ms = jnp.mean(y2 * y2, axis=0, keepdims=True)
        yn = y2 * lax.rsqrt(ms + EPS) * gssm_ref[...]
        mix_scr[tl * nc:(tl + 1) * nc, :] = yn.T.astype(BF16)

    for cp in gather(step, slot):
        cp.wait()
    t_per = FFN_ROWS // nc
    for sb in range(TB // t_per):
        rows = slice(sb * FFN_ROWS, (sb + 1) * FFN_ROWS)
        x_rows = jnp.concatenate([xbuf[slot, sb * t_per + j] for j in range(t_per)], axis=0)
        x1 = x_rows + _dot(mix_scr[rows, :], wo_ref[...])
        h2 = _rms_rows(x1, gffn_ref[...]).astype(BF16)
        x2 = x1
        for lo, hi in zip(FF_SPLIT[:-1], FF_SPLIT[1:]):
            gate = _dot(h2, wg_ref[:, lo:hi])
            up = _dot(h2, wu_ref[:, lo:hi])
            act = (gate * _sigmoid(gate) * up).astype(BF16)
            x2 = x2 + _dot(act, wd_ref[lo:hi, :])
        out = _rms_rows(x2, gfin_ref[...])
        for j in range(t_per):
            xbuf[slot, sb * t_per + j] = out[j * nc:(j + 1) * nc]

    for cp in scatter(step, slot):
        cp.start()

    @pl.when(step == n_steps - 1)
    def _():
        for back in range(min(TAIL_SLOTS, n_steps)):
            for cp in scatter(step - back, (step - back) % TAIL_SLOTS):
                cp.wait()


def _tail(yg, x4, wglut, bglu, gssm, wo_ssm, gffn, wg, wu, wd, gfin):
    nb, nc = x4.shape[0], x4.shape[1]
    n_tblocks = SSM_CHUNK // TB
    const = lambda b, i: (0, 0)

    def resident(shape):
        return pl.BlockSpec(shape, const, pipeline_mode=pl.Buffered(1))

    return pl.pallas_call(
        functools.partial(_tail_kernel, nc=nc, n_tblocks=n_tblocks, n_steps=nb * n_tblocks),
        grid=(nb, n_tblocks),
        in_specs=[
            pl.BlockSpec((N_GROUPS, TB * SSM_GROUP, nc), lambda b, i: (0, i, b)),
            pl.BlockSpec(memory_space=pl.ANY),
            resident((D_SSM, D_SSM)),
            resident((D_SSM, 1)),
            resident((D_SSM, 1)),
            resident((D_SSM, D_MODEL)),
            resident((1, D_MODEL)),
            resident((D_MODEL, D_FF)),
            resident((D_MODEL, D_FF)),
            resident((D_FF, D_MODEL)),
            resident((1, D_MODEL)),
        ],
        out_specs=pl.BlockSpec(memory_space=pl.ANY),
        out_shape=jax.ShapeDtypeStruct(x4.shape, F32),
        scratch_shapes=[pltpu.VMEM((TAIL_SLOTS, TB, nc, D_MODEL), F32),
                        pltpu.VMEM((TB * nc, D_SSM), BF16),
                        pltpu.SemaphoreType.DMA((TAIL_SLOTS,)),
                        pltpu.SemaphoreType.DMA((TAIL_SLOTS,))],
        compiler_params=pltpu.CompilerParams(dimension_semantics=("arbitrary", "arbitrary"),
                                             vmem_limit_bytes=VMEM_LIMIT),
        name="tail",
    )(yg, x4, wglut, bglu, gssm, wo_ssm, gffn, wg, wu, wd, gfin)


def _twice(a):
    return jnp.concatenate([a, a], axis=-1)


def kernel(x, norm_mix_g, w_in, ssm_lam_re, ssm_lam_im, ssm_log_dt, ssm_b_re, ssm_b_im, ssm_c_re,
           ssm_c_im, ssm_d, ssm_w_glu, ssm_b_glu, attn_rel_bias, norm_ssm_out_g, norm_att_out_g,
           w_out, norm_ffn_g, w_gate, w_up, w_down, norm_final_g):
    nb, seq, _ = x.shape
    nc = seq // SSM_CHUNK
    assert w_in.shape[0] == 1 and seq % Q_BLOCK == 0 and (nb * seq) % ROW_TILE == 0
    assert FFN_ROWS % nc == 0 and (TB * nc) % FFN_ROWS == 0
    w_in0 = w_in[0]
    scale = LOG2E / math.sqrt(HEAD_DIM)
    wqkv = jnp.concatenate([w_in0[:, D_SSM:D_SSM + D_ATT] * scale, w_in0[:, D_SSM + D_ATT:]],
                           axis=1).astype(BF16)
    wut = w_in0[:, :D_SSM].T.astype(BF16)
    wo = w_out[0].astype(BF16)
    g_mix = norm_mix_g[0][None, :]
    x4 = x.reshape(nb, nc, SSM_CHUNK, D_MODEL)
    lam2 = _twice(jnp.stack([ssm_lam_re[0], ssm_lam_im[0]], axis=1))
    b_re2, b_im2 = _twice(ssm_b_re[0].transpose(0, 2, 1)), _twice(ssm_b_im[0].transpose(0, 2, 1))
    c_re2, c_im2 = _twice(ssm_c_re[0]), _twice(ssm_c_im[0])
    lag0 = (SSM_CHUNK - 1) * SSM_GROUP
    d_pad = jnp.pad(ssm_d[0], ((0, 0), (lag0, LAG_COLS - lag0 - SSM_GROUP)))[:, None, :]

    q, k, v = _qkv(x.reshape(nb * seq, D_MODEL), g_mix, wqkv)
    ut = _ssm_in(x4, g_mix, wut)
    yg = _ssm_core(ut, lam2, ssm_log_dt[0][:, None, None], b_re2, b_im2, c_re2, c_im2, d_pad, nc)
    x1a = _attention(q.reshape(nb, seq, D_ATT), k.reshape(nb, seq, D_ATT), v.reshape(nb, seq, D_ATT),
                     jnp.pad(attn_rel_bias[0], ((0, 0), (0, REL_PAD - (2 * MAX_REL + 1)))),
                     norm_att_out_g[0][:, None], x, wo[D_SSM:])
    out4 = _tail(yg, x1a.reshape(nb, nc, SSM_CHUNK, D_MODEL),
                 ssm_w_glu[0].T.astype(BF16), ssm_b_glu[0][:, None], norm_ssm_out_g[0][:, None],
                 wo[:D_SSM], norm_ffn_g[0][None, :],
                 w_gate[0].astype(BF16), w_up[0].astype(BF16), w_down[0].astype(BF16),
                 norm_final_g[None, :])
    return out4.reshape(nb, seq, D_MODEL)
```

```python
import functools
import math

import jax
import jax.numpy as jnp
from jax import lax
from jax.experimental import pallas as pl
from jax.experimental.pallas import tpu as pltpu

F32 = jnp.float32
BF16 = jnp.bfloat16
HIGHEST = lax.Precision.HIGHEST

D_MODEL = 1024
D_SSM = 512
D_ATT = 512
SSM_GROUP = 16
N_GROUPS = D_SSM // SSM_GROUP
SSM_STATE = 64
HEAD_DIM = 64
N_HEADS = D_ATT // HEAD_DIM
ATT_CHUNK = 64
LEFT_CHUNKS = 8
MAX_REL = 128
D_FF = 2816
EPS = 1e-6
MASK_VALUE = -1e30
LOG2E = math.log2(math.e)

SSM_CHUNK = 64
SSM_ROWS = SSM_CHUNK * SSM_GROUP
TOEP_TILE = 256
TOEP_T = TOEP_TILE // SSM_GROUP
TOEP_TILES = SSM_ROWS // TOEP_TILE
LAG_STEPS = SSM_CHUNK + TOEP_T
LAG_COLS = LAG_STEPS * SSM_GROUP
POW_ROWS = 72
Q_BLOCK = 256
HALF_Q = 128
HEADS_PER_PROJ = 4
KEY_BLOCKS = 3
REL_WIDTH = (KEY_BLOCKS + 1) * Q_BLOCK
REL_PAD = 384
ROW_TILE = 1024
TB = 8
FFN_ROWS = 512
FF_SPLIT = (0, 1536, D_FF)
IN_SLOTS = 2
TAIL_SLOTS = 3
VMEM_LIMIT = 56 * 1024 * 1024


def _sigmoid(z):
    return 1.0 / (1.0 + jnp.exp(-z))


def _rms_rows(x, g_row):
    ms = jnp.mean(x * x, axis=-1, keepdims=True)
    return x * lax.rsqrt(ms + EPS) * g_row


def _dot(a, b, precision=None):
    return jnp.dot(a, b, preferred_element_type=F32, precision=precision)


def _dot_nt(a, b, precision=None):
    return lax.dot_general(a, b, (((1,), (1,)), ((), ())), preferred_element_type=F32, precision=precision)


def _strided_copies(hbm, buf, sem, step, slot, n_tblocks, to_hbm=False):
    b, i = step // n_tblocks, step % n_tblocks
    copies = []
    for tl in range(TB):
        src, dst = hbm.at[b, :, i * TB + tl, :], buf.at[slot, tl]
        if to_hbm:
            src, dst = dst, src
        copies.append(pltpu.make_async_copy(src, dst, sem.at[slot]))
    return copies


def _qkv_kernel(x_ref, g_ref, wqk_ref, wvt_ref, q_ref, k_ref, vt_ref):
    h = _rms_rows(x_ref[...], g_ref[...]).astype(BF16)
    qk = _dot(h, wqk_ref[...])
    q_ref[...] = qk[:, 0:D_ATT].astype(BF16)
    k_ref[...] = qk[:, D_ATT:2 * D_ATT].astype(BF16)
    vt_ref[...] = _dot_nt(wvt_ref[...], h).astype(BF16)


def _qkv(x2, g_row, wqk, wvt):
    n_rows = x2.shape[0]
    const = lambda i: (0, 0)
    out_shape = jax.ShapeDtypeStruct((n_rows, D_ATT), BF16)
    out_spec = pl.BlockSpec((ROW_TILE, D_ATT), lambda i: (i, 0))
    return pl.pallas_call(
        _qkv_kernel,
        grid=(n_rows // ROW_TILE,),
        in_specs=[pl.BlockSpec((ROW_TILE, D_MODEL), lambda i: (i, 0)),
                  pl.BlockSpec((1, D_MODEL), const),
                  pl.BlockSpec((D_MODEL, 2 * D_ATT), const),
                  pl.BlockSpec((D_ATT, D_MODEL), const)],
        out_specs=[out_spec, out_spec, pl.BlockSpec((D_ATT, ROW_TILE), lambda i: (0, i))],
        out_shape=[out_shape, out_shape, jax.ShapeDtypeStruct((D_ATT, n_rows), BF16)],
        compiler_params=pltpu.CompilerParams(dimension_semantics=("arbitrary",),
                                             vmem_limit_bytes=VMEM_LIMIT),
        name="qkv",
    )(x2, g_row, wqk, wvt)


def _ssm_in_kernel(x_hbm, g_ref, wut_ref, ut_ref, xbuf, h_scr, sem, *, nc, n_tblocks, n_steps):
    step = pl.program_id(0) * n_tblocks + pl.program_id(1)
    slot = step % IN_SLOTS

    @pl.when(step == 0)
    def _():
        for cp in _strided_copies(x_hbm, xbuf, sem, step, slot, n_tblocks):
            cp.start()

    @pl.when(step + 1 < n_steps)
    def _():
        for cp in _strided_copies(x_hbm, xbuf, sem, step + 1, (step + 1) % IN_SLOTS, n_tblocks):
            cp.start()

    for cp in _strided_copies(x_hbm, xbuf, sem, step, slot, n_tblocks):
        cp.wait()
    for tl in range(TB):
        h_scr[tl * nc:(tl + 1) * nc, :] = _rms_rows(xbuf[slot, tl], g_ref[...]).astype(BF16)
    ut = _dot_nt(wut_ref[...], h_scr[...])
    for tl in range(TB):
        ut_ref[:, tl * SSM_GROUP:(tl + 1) * SSM_GROUP, :] = (
            ut[:, tl * nc:(tl + 1) * nc].reshape(N_GROUPS, SSM_GROUP, nc).astype(BF16))


def _ssm_in(x4, g_row, wut):
    nb, nc = x4.shape[0], x4.shape[1]
    n_tblocks = SSM_CHUNK // TB
    return pl.pallas_call(
        functools.partial(_ssm_in_kernel, nc=nc, n_tblocks=n_tblocks, n_steps=nb * n_tblocks),
        grid=(nb, n_tblocks),
        in_specs=[pl.BlockSpec(memory_space=pl.ANY),
                  pl.BlockSpec((1, D_MODEL), lambda b, i: (0, 0)),
                  pl.BlockSpec((D_SSM, D_MODEL), lambda b, i: (0, 0))],
        out_specs=pl.BlockSpec((N_GROUPS, TB * SSM_GROUP, nc), lambda b, i: (0, i, b)),
        out_shape=jax.ShapeDtypeStruct((N_GROUPS, SSM_ROWS, nb * nc), BF16),
        scratch_shapes=[pltpu.VMEM((IN_SLOTS, TB, nc, D_MODEL), F32),
                        pltpu.VMEM((TB * nc, D_MODEL), BF16),
                        pltpu.SemaphoreType.DMA((IN_SLOTS,))],
        compiler_params=pltpu.CompilerParams(dimension_semantics=("arbitrary", "arbitrary"),
                                             vmem_limit_bytes=VMEM_LIMIT),
        name="ssm_in",
    )(x4, g_row, wut)


def _cexp(re, im):
    mag = jnp.exp(re)
    return mag * jnp.cos(im), mag * jnp.sin(im)


def _ssm_kernel(ut_ref, lam_ref, ldt_ref, bre_ref, bim_ref, cre_ref, cim_ref, d_ref,
                yg_ref, mt_scr, cpow_scr, tcol_scr, *, nc, n_steps):
    n_p, n_h = SSM_STATE, SSM_GROUP
    dt = jnp.exp(ldt_ref[0])
    lam_r, lam_i = lam_ref[0, 0:1, :], lam_ref[0, 1:2, :]
    ar, ai = lam_r * dt, lam_i * dt
    lbr, lbi = _cexp(ar, ai)
    den = lam_r * lam_r + lam_i * lam_i
    fr = ((lbr - 1.0) * lam_r + lbi * lam_i) / den
    fi = (lbi * lam_r - (lbr - 1.0) * lam_i) / den
    b_re, b_im = bre_ref[0], bim_ref[0]
    bbr, bbi = fr * b_re - fi * b_im, fr * b_im + fi * b_re
    c_re, c_im = cre_ref[0], cim_ref[0]
    jrow = lax.broadcasted_iota(jnp.int32, (POW_ROWS, 1), 0).astype(F32)
    pwr, pwi = _cexp(ar * jrow, ai * jrow)
    re_half = lax.broadcasted_iota(jnp.int32, (n_h, 2 * n_p), 1) < n_p
    for m in range(SSM_CHUNK):
        j = SSM_CHUNK - 1 - m
        pr, pi = pwr[j:j + 1, :], pwi[j:j + 1, :]
        mt_scr[m * n_h:(m + 1) * n_h, :] = jnp.where(re_half, bbr * pr - bbi * pi, bbr * pi + bbi * pr)
    mt_scr[SSM_ROWS:, :] = jnp.zeros((LAG_COLS - SSM_ROWS, 2 * n_p), F32)
    for t in range(SSM_CHUNK):
        pr, pi = pwr[t + 1:t + 2, :], pwi[t + 1:t + 2, :]
        cpow_scr[t * n_h:(t + 1) * n_h, :] = jnp.where(
            re_half, c_re * pr - c_im * pi, -(c_re * pi + c_im * pr)).astype(BF16)
    mt = mt_scr[...]
    lagk = _dot_nt(jnp.where(re_half, c_re, -c_im), mt, HIGHEST)
    col_h = lax.broadcasted_iota(jnp.int32, (n_h, LAG_COLS), 1)
    row_h = lax.broadcasted_iota(jnp.int32, (n_h, LAG_COLS), 0)
    lagk = lagk + jnp.where(col_h % n_h == row_h, d_ref[0], 0.0)
    for t_out in range(SSM_CHUNK):
        start = (SSM_CHUNK - 1 - t_out) * n_h
        tcol_scr[t_out * n_h:(t_out + 1) * n_h, :] = lagk[:, start:start + TOEP_TILE].astype(BF16)
    u = ut_ref[0]
    lanes = u.shape[1]
    v = _dot(mt[:SSM_ROWS, :].T.astype(BF16), u)
    xr, xi = v[:n_p], v[n_p:]
    eye = (lax.broadcasted_iota(jnp.int32, (n_p, 2 * n_p), 0)
           == lax.broadcasted_iota(jnp.int32, (n_p, 2 * n_p), 1))
    a_r = jnp.sum(jnp.where(eye, pwr[SSM_CHUNK:SSM_CHUNK + 1, :], 0.0), axis=1, keepdims=True)
    a_i = jnp.sum(jnp.where(eye, pwi[SSM_CHUNK:SSM_CHUNK + 1, :], 0.0), axis=1, keepdims=True)
    chunk = lax.broadcasted_iota(jnp.int32, (n_p, lanes), 1) % nc
    for k in range(n_steps):
        sh = 1 << k
        keep = chunk >= sh
        sr = jnp.where(keep, pltpu.roll(xr, sh, 1), 0.0)
        si = jnp.where(keep, pltpu.roll(xi, sh, 1), 0.0)
        xr, xi = xr + (a_r * sr - a_i * si), xi + (a_r * si + a_i * sr)
        a_r, a_i = a_r * a_r - a_i * a_i, 2.0 * a_r * a_i
    keep = chunk >= 1
    pr = jnp.where(keep, pltpu.roll(xr, 1, 1), 0.0)
    pi = jnp.where(keep, pltpu.roll(xi, 1, 1), 0.0)
    xprev = jnp.concatenate([pr, pi], axis=0).astype(BF16)
    for bi in range(TOEP_TILES):
        rows = slice(bi * TOEP_TILE, (bi + 1) * TOEP_TILE)
        acc = _dot(cpow_scr[rows, :], xprev)
        for bj in range(bi + 1):
            d = bi - bj
            acc = acc + _dot(tcol_scr[d * TOEP_TILE:(d + 1) * TOEP_TILE, :],
                             u[bj * TOEP_TILE:(bj + 1) * TOEP_TILE, :])
        yg_ref[0, rows, :] = jax.nn.gelu(acc).astype(BF16)


def _ssm_core(ut, lam2, log_dt, b_re, b_im, c_re, c_im, d_pad, nc):
    lanes = ut.shape[2]
    n_steps = max(1, (nc - 1).bit_length())
    g3 = lambda g: (g, 0, 0)
    par = pl.BlockSpec((1, SSM_GROUP, 2 * SSM_STATE), g3)
    return pl.pallas_call(
        functools.partial(_ssm_kernel, nc=nc, n_steps=n_steps),
        grid=(N_GROUPS,),
        in_specs=[
            pl.BlockSpec((1, SSM_ROWS, lanes), g3),
            pl.BlockSpec((1, 2, 2 * SSM_STATE), g3),
            pl.BlockSpec((1, 1, 1), g3),
            par, par, par, par,
            pl.BlockSpec((1, 1, LAG_COLS), g3),
        ],
        out_specs=pl.BlockSpec((1, SSM_ROWS, lanes), g3),
        out_shape=jax.ShapeDtypeStruct((N_GROUPS, SSM_ROWS, lanes), BF16),
        scratch_shapes=[pltpu.VMEM((LAG_COLS, 2 * SSM_STATE), F32),
                        pltpu.VMEM((SSM_ROWS, 2 * SSM_STATE), BF16),
                        pltpu.VMEM((SSM_ROWS, TOEP_TILE), BF16)],
        compiler_params=pltpu.CompilerParams(dimension_semantics=("arbitrary",),
                                             vmem_limit_bytes=VMEM_LIMIT),
        name="ssm_core",
    )(ut, lam2, log_dt, b_re, b_im, c_re, c_im, d_pad)


def _build_bias_table(rb_ref, tab_scr):
    n_keys, width = KEY_BLOCKS * Q_BLOCK, REL_WIDTH
    s = lax.broadcasted_iota(jnp.int32, (1, width), 1)
    delta = jnp.where(s < Q_BLOCK, -s, width - s)
    rel = delta - (KEY_BLOCKS - 1) * Q_BLOCK
    idx = jnp.clip(rel, -MAX_REL, MAX_REL) + MAX_REL
    pick = (lax.broadcasted_iota(jnp.int32, (REL_PAD, width), 0) == idx).astype(F32)
    w = _dot(rb_ref[...], pick, HIGHEST)
    kk = lax.broadcasted_iota(jnp.int32, (n_keys, Q_BLOCK), 0)
    qq = lax.broadcasted_iota(jnp.int32, (n_keys, Q_BLOCK), 1)
    kc = kk // ATT_CHUNK - (KEY_BLOCKS - 1) * (Q_BLOCK // ATT_CHUNK)
    qc = qq // ATT_CHUNK
    valid = (kc <= qc) & (kc >= qc - LEFT_CHUNKS)
    for h in range(N_HEADS):
        rows = jnp.broadcast_to(w[h:h + 1, :], (n_keys, width))
        toep = pltpu.roll(rows, 0, 1, stride=1, stride_axis=0)
        tab_scr[h] = jnp.where(valid, toep[:, :Q_BLOCK] * LOG2E, MASK_VALUE)


def _attn_kernel(q_ref, k0_ref, k1_ref, k2_ref, v0_ref, v1_ref, v2_ref, rb_ref, g_ref, x_ref, wo_ref,
                 o_ref, tab_scr):
    i = pl.program_id(1)
    kv_all = ((k0_ref, v0_ref, 0), (k1_ref, v1_ref, 1), (k2_ref, v2_ref, 2))

    @pl.when((pl.program_id(0) == 0) & (i == 0))
    def _():
        _build_bias_table(rb_ref, tab_scr)

    def body(kv_list):
        q = q_ref[0]
        lane = lax.broadcasted_iota(jnp.int32, (Q_BLOCK, 2 * HEAD_DIM), 1)

        def scores(h):
            pair = slice((h // 2) * 2 * HEAD_DIM, (h // 2 + 1) * 2 * HEAD_DIM)
            qp = q[:, pair]
            qh = jnp.where((lane < HEAD_DIM) if h % 2 == 0 else (lane >= HEAD_DIM), qp, jnp.zeros_like(qp))
            return [_dot_nt(k_ref[0, :, pair], qh) for (k_ref, _, _) in kv_list]

        def probs(h, raw, half):
            cols = slice(half * HALF_Q, (half + 1) * HALF_Q)
            band_lo = half * HALF_Q
            band_hi = band_lo + (LEFT_CHUNKS + HALF_Q // ATT_CHUNK) * ATT_CHUNK
            parts = []
            for s, (_, _, rel) in zip(raw, kv_list):
                lo, hi = max(band_lo - rel * Q_BLOCK, 0), min(band_hi - rel * Q_BLOCK, Q_BLOCK)
                parts.append((lo, hi, s[lo:hi, cols] + tab_scr[h, rel * Q_BLOCK + lo:rel * Q_BLOCK + hi, cols]))
            m = parts[0][2].max(axis=0, keepdims=True)
            for _, _, s in parts[1:]:
                m = jnp.maximum(m, s.max(axis=0, keepdims=True))
            denom, out = None, []
            for lo, hi, s in parts:
                p = jnp.exp2(s - m)
                ps = p.sum(axis=0, keepdims=True)
                denom = ps if denom is None else denom + ps
                p = p.astype(BF16)
                pad = [jnp.zeros((n, HALF_Q), BF16) for n in (lo, Q_BLOCK - hi)]
                out.append(jnp.concatenate([z for z in (pad[0], p, pad[1]) if z.shape[0]], axis=0))
            return out, denom

        raw_next = scores(0)
        heads, proj, ssq = [], None, None
        for h in range(N_HEADS):
            raw = raw_next
            if h + 1 < N_HEADS:
                raw_next = scores(h + 1)
            (p_lo, d_lo), (p_hi, d_hi) = probs(h, raw, 0), probs(h, raw, 1)
            o = None
            for pl_, ph_, (_, vt_ref, _) in zip(p_lo, p_hi, kv_list):
                oj = _dot(vt_ref[h * HEAD_DIM:(h + 1) * HEAD_DIM, :], jnp.concatenate([pl_, ph_], axis=1))
                o = oj if o is None else o + oj
            heads.append(o / jnp.concatenate([d_lo, d_hi], axis=1))
            if len(heads) == HEADS_PER_PROJ:
                cols = slice((h + 1 - HEADS_PER_PROJ) * HEAD_DIM, (h + 1) * HEAD_DIM)
                ot = jnp.concatenate(heads, axis=0).T
                sq = jnp.sum(ot * ot, axis=1, keepdims=True)
                part = _dot((ot * g_ref[:, cols]).astype(BF16), wo_ref[cols, :])
                proj, ssq = (part, sq) if proj is None else (proj + part, ssq + sq)
                heads = []
        o_ref[0] = x_ref[0] + proj * lax.rsqrt(ssq / D_ATT + EPS)

    pl.when(i == 0)(lambda: body(kv_all[2:]))
    pl.when(i == 1)(lambda: body(kv_all[1:]))
    pl.when(i >= 2)(lambda: body(kv_all))


def _attention(q, k, vt, rb_pad, g_row, x, wo_att):
    nb, seq, _ = q.shape
    n_blocks = seq // Q_BLOCK
    blk = (1, Q_BLOCK, D_ATT)
    vblk = (D_ATT, Q_BLOCK)
    xblk = (1, Q_BLOCK, D_MODEL)

    def back(n):
        return lambda b, i: (b, jnp.maximum(i - n, 0), 0)

    def back_t(n):
        return lambda b, i: (0, b * n_blocks + jnp.maximum(i - n, 0))

    return pl.pallas_call(
        _attn_kernel,
        grid=(nb, seq // Q_BLOCK),
        in_specs=[
            pl.BlockSpec(blk, back(0)),
            pl.BlockSpec(blk, back(2)), pl.BlockSpec(blk, back(1)), pl.BlockSpec(blk, back(0)),
            pl.BlockSpec(vblk, back_t(2)), pl.BlockSpec(vblk, back_t(1)), pl.BlockSpec(vblk, back_t(0)),
            pl.BlockSpec((N_HEADS, REL_PAD), lambda b, i: (0, 0)),
            pl.BlockSpec((1, D_ATT), lambda b, i: (0, 0)),
            pl.BlockSpec(xblk, back(0)),
            pl.BlockSpec((D_ATT, D_MODEL), lambda b, i: (0, 0)),
        ],
        out_specs=pl.BlockSpec(xblk, back(0)),
        out_shape=jax.ShapeDtypeStruct((nb, seq, D_MODEL), F32),
        scratch_shapes=[pltpu.VMEM((N_HEADS, KEY_BLOCKS * Q_BLOCK, Q_BLOCK), F32)],
        compiler_params=pltpu.CompilerParams(dimension_semantics=("arbitrary", "arbitrary"),
                                             vmem_limit_bytes=VMEM_LIMIT),
        name="chunk_attn",
    )(q, k, k, k, vt, vt, vt, rb_pad, g_row, x, wo_att)


def _tail_kernel(yg_ref, x_hbm, wglut_ref, bglu_ref, gssm_ref, wo_ref, gffn_ref, wg_ref, wu_ref, wd_ref,
                 gfin_ref, o_hbm, xbuf, mix_scr, in_sem, out_sem, *, nc, n_tblocks, n_steps):
    step = pl.program_id(0) * n_tblocks + pl.program_id(1)
    slot = step % TAIL_SLOTS
    gather = functools.partial(_strided_copies, x_hbm, xbuf, in_sem, n_tblocks=n_tblocks)
    scatter = functools.partial(_strided_copies, o_hbm, xbuf, out_sem, n_tblocks=n_tblocks, to_hbm=True)

    @pl.when(step == 0)
    def _():
        for cp in gather(step, slot):
            cp.start()

    @pl.when(step + 1 < n_steps)
    def _():
        nxt = (step + 1) % TAIL_SLOTS

        @pl.when(step + 1 >= TAIL_SLOTS)
        def _():
            for cp in scatter(step + 1 - TAIL_SLOTS, nxt):
                cp.wait()

        for cp in gather(step + 1, nxt):
            cp.start()

    for tl in range(TB):
        yg = yg_ref[:, tl * SSM_GROUP:(tl + 1) * SSM_GROUP, :].reshape(D_SSM, nc)
        z = _dot(wglut_ref[...], yg) + bglu_ref[...]
        y2 = yg.astype(F32) * _sigmoid(z)
        ms = jnp.mean(y2 * y2, axis=0, keepdims=True)
        yn = y2 * lax.rsqrt(ms + EPS) * gssm_ref[...]
        mix_scr[tl * nc:(tl + 1) * nc, :] = yn.T.astype(BF16)

    for cp in gather(step, slot):
        cp.wait()
    t_per = FFN_ROWS // nc
    for sb in range(TB // t_per):
        rows = slice(sb * FFN_ROWS, (sb + 1) * FFN_ROWS)
        x_rows = jnp.concatenate([xbuf[slot, sb * t_per + j] for j in range(t_per)], axis=0)
        x1 = x_rows + _dot(mix_scr[rows, :], wo_ref[...])
        h2 = _rms_rows(x1, gffn_ref[...]).astype(BF16)
        x2 = x1
        for lo, hi in zip(FF_SPLIT[:-1], FF_SPLIT[1:]):
            gate = _dot(h2, wg_ref[:, lo:hi])
            up = _dot(h2, wu_ref[:, lo:hi])
            act = (gate * _sigmoid(gate) * up).astype(BF16)
            x2 = x2 + _dot(act, wd_ref[lo:hi, :])
        out = _rms_rows(x2, gfin_ref[...])
        for j in range(t_per):
            xbuf[slot, sb * t_per + j] = out[j * nc:(j + 1) * nc]

    for cp in scatter(step, slot):
        cp.start()

    @pl.when(step == n_steps - 1)
    def _():
        for back in range(min(TAIL_SLOTS, n_steps)):
            for cp in scatter(step - back, (step - back) % TAIL_SLOTS):
                cp.wait()


def _tail(yg, x4, wglut, bglu, gssm, wo_ssm, gffn, wg, wu, wd, gfin):
    nb, nc = x4.shape[0], x4.shape[1]
    n_tblocks = SSM_CHUNK // TB
    const = lambda b, i: (0, 0)

    def resident(shape):
        return pl.BlockSpec(shape, const, pipeline_mode=pl.Buffered(1))

    return pl.pallas_call(
        functools.partial(_tail_kernel, nc=nc, n_tblocks=n_tblocks, n_steps=nb * n_tblocks),
        grid=(nb, n_tblocks),
        in_specs=[
            pl.BlockSpec((N_GROUPS, TB * SSM_GROUP, nc), lambda b, i: (0, i, b)),
            pl.BlockSpec(memory_space=pl.ANY),
            resident((D_SSM, D_SSM)),
            resident((D_SSM, 1)),
            resident((D_SSM, 1)),
            resident((D_SSM, D_MODEL)),
            resident((1, D_MODEL)),
            resident((D_MODEL, D_FF)),
            resident((D_MODEL, D_FF)),
            resident((D_FF, D_MODEL)),
            resident((1, D_MODEL)),
        ],
        out_specs=pl.BlockSpec(memory_space=pl.ANY),
        out_shape=jax.ShapeDtypeStruct(x4.shape, F32),
        scratch_shapes=[pltpu.VMEM((TAIL_SLOTS, TB, nc, D_MODEL), F32),
                        pltpu.VMEM((TB * nc, D_SSM), BF16),
                        pltpu.SemaphoreType.DMA((TAIL_SLOTS,)),
                        pltpu.SemaphoreType.DMA((TAIL_SLOTS,))],
        compiler_params=pltpu.CompilerParams(dimension_semantics=("arbitrary", "arbitrary"),
                                             vmem_limit_bytes=VMEM_LIMIT),
        name="tail",
    )(yg, x4, wglut, bglu, gssm, wo_ssm, gffn, wg, wu, wd, gfin)


def _twice(a):
    return jnp.concatenate([a, a], axis=-1)


def kernel(x, norm_mix_g, w_in, ssm_lam_re, ssm_lam_im, ssm_log_dt, ssm_b_re, ssm_b_im, ssm_c_re,
           ssm_c_im, ssm_d, ssm_w_glu, ssm_b_glu, attn_rel_bias, norm_ssm_out_g, norm_att_out_g,
           w_out, norm_ffn_g, w_gate, w_up, w_down, norm_final_g):
    nb, seq, _ = x.shape
    nc = seq // SSM_CHUNK
    assert w_in.shape[0] == 1 and seq % Q_BLOCK == 0 and (nb * seq) % ROW_TILE == 0
    assert FFN_ROWS % nc == 0 and (TB * nc) % FFN_ROWS == 0
    w_in0 = w_in[0]
    scale = LOG2E / math.sqrt(HEAD_DIM)
    wqk = jnp.concatenate([w_in0[:, D_SSM:D_SSM + D_ATT] * scale, w_in0[:, D_SSM + D_ATT:D_SSM + 2 * D_ATT]],
                          axis=1).astype(BF16)
    wvt = w_in0[:, D_SSM + 2 * D_ATT:].T.astype(BF16)
    wut = w_in0[:, :D_SSM].T.astype(BF16)
    wo = w_out[0].astype(BF16)
    g_mix = norm_mix_g[0][None, :]
    x4 = x.reshape(nb, nc, SSM_CHUNK, D_MODEL)
    lam2 = _twice(jnp.stack([ssm_lam_re[0], ssm_lam_im[0]], axis=1))
    b_re2, b_im2 = _twice(ssm_b_re[0].transpose(0, 2, 1)), _twice(ssm_b_im[0].transpose(0, 2, 1))
    c_re2, c_im2 = _twice(ssm_c_re[0]), _twice(ssm_c_im[0])
    lag0 = (SSM_CHUNK - 1) * SSM_GROUP
    d_pad = jnp.pad(ssm_d[0], ((0, 0), (lag0, LAG_COLS - lag0 - SSM_GROUP)))[:, None, :]

    q, k, vt = _qkv(x.reshape(nb * seq, D_MODEL), g_mix, wqk, wvt)
    ut = _ssm_in(x4, g_mix, wut)
    yg = _ssm_core(ut, lam2, ssm_log_dt[0][:, None, None], b_re2, b_im2, c_re2, c_im2, d_pad, nc)
    x1a = _attention(q.reshape(nb, seq, D_ATT), k.reshape(nb, seq, D_ATT), vt,
                     jnp.pad(attn_rel_bias[0], ((0, 0), (0, REL_PAD - (2 * MAX_REL + 1)))),
                     norm_att_out_g[0][None, :], x, wo[D_SSM:])
    out4 = _tail(yg, x1a.reshape(nb, nc, SSM_CHUNK, D_MODEL),
                 ssm_w_glu[0].T.astype(BF16), ssm_b_glu[0][:, None], norm_ssm_out_g[0][:, None],
                 wo[:D_SSM], norm_ffn_g[0][None, :],
                 w_gate[0].astype(BF16), w_up[0].astype(BF16), w_down[0].astype(BF16),
                 norm_final_g[None, :])
    return out4.reshape(nb, seq, D_MODEL)
```

```python
import functools
import math

import jax
import jax.numpy as jnp
from jax import lax
from jax.experimental import pallas as pl
from jax.experimental.pallas import tpu as pltpu

F32 = jnp.float32
BF16 = jnp.bfloat16
HIGHEST = lax.Precision.HIGHEST

D_MODEL = 1024
D_SSM = 512
D_ATT = 512
SSM_GROUP = 16
N_GROUPS = D_SSM // SSM_GROUP
SSM_STATE = 64
HEAD_DIM = 64
N_HEADS = D_ATT // HEAD_DIM
ATT_CHUNK = 64
LEFT_CHUNKS = 8
MAX_REL = 128
D_FF = 2816
EPS = 1e-6
MASK_VALUE = -1e30
LOG2E = math.log2(math.e)

SSM_CHUNK = 64
SSM_ROWS = SSM_CHUNK * SSM_GROUP
TOEP_TILE = 256
TOEP_T = TOEP_TILE // SSM_GROUP
TOEP_TILES = SSM_ROWS // TOEP_TILE
LAG_STEPS = SSM_CHUNK + TOEP_T
LAG_COLS = LAG_STEPS * SSM_GROUP
POW_ROWS = 72
GROUPS_PER_STEP = 4
Q_BLOCK = 256
HALF_Q = 128
HEADS_PER_PROJ = 4
KEY_BLOCKS = 3
REL_WIDTH = (KEY_BLOCKS + 1) * Q_BLOCK
REL_PAD = 384
ROW_TILE = 1024
TB = 8
FFN_ROWS = 512
FF_SPLIT = (0, 1536, D_FF)
IN_SLOTS = 2
TAIL_SLOTS = 3
VMEM_LIMIT = 56 * 1024 * 1024


def _sigmoid(z):
    return 1.0 / (1.0 + jnp.exp(-z))


def _rms_rows(x, g_row):
    ms = jnp.mean(x * x, axis=-1, keepdims=True)
    return x * lax.rsqrt(ms + EPS) * g_row


def _dot(a, b, precision=None):
    return jnp.dot(a, b, preferred_element_type=F32, precision=precision)


def _dot_nt(a, b, precision=None):
    return lax.dot_general(a, b, (((1,), (1,)), ((), ())), preferred_element_type=F32, precision=precision)


def _strided_copies(hbm, buf, sem, step, slot, n_tblocks, to_hbm=False):
    b, i = step // n_tblocks, step % n_tblocks
    copies = []
    for tl in range(TB):
        src, dst = hbm.at[b, :, i * TB + tl, :], buf.at[slot, tl]
        if to_hbm:
            src, dst = dst, src
        copies.append(pltpu.make_async_copy(src, dst, sem.at[slot]))
    return copies


def _qkv_kernel(x_ref, g_ref, wqk_ref, wvt_ref, q_ref, k_ref, vt_ref):
    h = _rms_rows(x_ref[...], g_ref[...]).astype(BF16)
    qk = _dot(h, wqk_ref[...])
    q_ref[...] = qk[:, 0:D_ATT].astype(BF16)
    k_ref[...] = qk[:, D_ATT:2 * D_ATT].astype(BF16)
    vt_ref[...] = _dot_nt(wvt_ref[...], h).astype(BF16)


def _qkv(x2, g_row, wqk, wvt):
    n_rows = x2.shape[0]
    const = lambda i: (0, 0)
    out_shape = jax.ShapeDtypeStruct((n_rows, D_ATT), BF16)
    out_spec = pl.BlockSpec((ROW_TILE, D_ATT), lambda i: (i, 0))
    return pl.pallas_call(
        _qkv_kernel,
        grid=(n_rows // ROW_TILE,),
        in_specs=[pl.BlockSpec((ROW_TILE, D_MODEL), lambda i: (i, 0)),
                  pl.BlockSpec((1, D_MODEL), const),
                  pl.BlockSpec((D_MODEL, 2 * D_ATT), const),
                  pl.BlockSpec((D_ATT, D_MODEL), const)],
        out_specs=[out_spec, out_spec, pl.BlockSpec((D_ATT, ROW_TILE), lambda i: (0, i))],
        out_shape=[out_shape, out_shape, jax.ShapeDtypeStruct((D_ATT, n_rows), BF16)],
        compiler_params=pltpu.CompilerParams(dimension_semantics=("arbitrary",),
                                             vmem_limit_bytes=VMEM_LIMIT),
        name="qkv",
    )(x2, g_row, wqk, wvt)


def _ssm_in_kernel(x_hbm, g_ref, wut_ref, ut_ref, xbuf, h_scr, sem, *, nc, n_tblocks, n_steps):
    step = pl.program_id(0) * n_tblocks + pl.program_id(1)
    slot = step % IN_SLOTS

    @pl.when(step == 0)
    def _():
        for cp in _strided_copies(x_hbm, xbuf, sem, step, slot, n_tblocks):
            cp.start()

    @pl.when(step + 1 < n_steps)
    def _():
        for cp in _strided_copies(x_hbm, xbuf, sem, step + 1, (step + 1) % IN_SLOTS, n_tblocks):
            cp.start()

    for cp in _strided_copies(x_hbm, xbuf, sem, step, slot, n_tblocks):
        cp.wait()
    for tl in range(TB):
        h_scr[tl * nc:(tl + 1) * nc, :] = _rms_rows(xbuf[slot, tl], g_ref[...]).astype(BF16)
    ut = _dot_nt(wut_ref[...], h_scr[...])
    for tl in range(TB):
        ut_ref[:, tl * SSM_GROUP:(tl + 1) * SSM_GROUP, :] = (
            ut[:, tl * nc:(tl + 1) * nc].reshape(N_GROUPS, SSM_GROUP, nc).astype(BF16))


def _ssm_in(x4, g_row, wut):
    nb, nc = x4.shape[0], x4.shape[1]
    n_tblocks = SSM_CHUNK // TB
    return pl.pallas_call(
        functools.partial(_ssm_in_kernel, nc=nc, n_tblocks=n_tblocks, n_steps=nb * n_tblocks),
        grid=(nb, n_tblocks),
        in_specs=[pl.BlockSpec(memory_space=pl.ANY),
                  pl.BlockSpec((1, D_MODEL), lambda b, i: (0, 0)),
                  pl.BlockSpec((D_SSM, D_MODEL), lambda b, i: (0, 0))],
        out_specs=pl.BlockSpec((N_GROUPS, TB * SSM_GROUP, nc), lambda b, i: (0, i, b)),
        out_shape=jax.ShapeDtypeStruct((N_GROUPS, SSM_ROWS, nb * nc), BF16),
        scratch_shapes=[pltpu.VMEM((IN_SLOTS, TB, nc, D_MODEL), F32),
                        pltpu.VMEM((TB * nc, D_MODEL), BF16),
                        pltpu.SemaphoreType.DMA((IN_SLOTS,))],
        compiler_params=pltpu.CompilerParams(dimension_semantics=("arbitrary", "arbitrary"),
                                             vmem_limit_bytes=VMEM_LIMIT),
        name="ssm_in",
    )(x4, g_row, wut)


def _cexp(re, im):
    mag = jnp.exp(re)
    return mag * jnp.cos(im), mag * jnp.sin(im)


def _ssm_group(ut_ref, lam_ref, ldt_ref, bre_ref, bim_ref, cre_ref, cim_ref, d_ref,
               yg_ref, mt_scr, cpow_scr, tcol_scr, *, nc, n_steps):
    n_p, n_h = SSM_STATE, SSM_GROUP
    dt = jnp.exp(ldt_ref[0])
    lam_r, lam_i = lam_ref[0, 0:1, :], lam_ref[0, 1:2, :]
    ar, ai = lam_r * dt, lam_i * dt
    lbr, lbi = _cexp(ar, ai)
    den = lam_r * lam_r + lam_i * lam_i
    fr = ((lbr - 1.0) * lam_r + lbi * lam_i) / den
    fi = (lbi * lam_r - (lbr - 1.0) * lam_i) / den
    b_re, b_im = bre_ref[0], bim_ref[0]
    bbr, bbi = fr * b_re - fi * b_im, fr * b_im + fi * b_re
    c_re, c_im = cre_ref[0], cim_ref[0]
    n = 8
    jrow = lax.broadcasted_iota(jnp.int32, (n, 1), 0).astype(F32)
    pwr, pwi = _cexp(ar * jrow, ai * jrow)
    er, ei = _cexp(ar * float(n), ai * float(n))
    while n < POW_ROWS:
        take = min(n, POW_ROWS - n)
        pwr, pwi = (jnp.concatenate([pwr, pwr[:take] * er - pwi[:take] * ei], axis=0),
                    jnp.concatenate([pwi, pwr[:take] * ei + pwi[:take] * er], axis=0))
        er, ei = er * er - ei * ei, 2.0 * er * ei
        n += take
    re_half = lax.broadcasted_iota(jnp.int32, (n_h, 2 * n_p), 1) < n_p
    for m in range(SSM_CHUNK):
        j = SSM_CHUNK - 1 - m
        pr, pi = pwr[j:j + 1, :], pwi[j:j + 1, :]
        mt_scr[m * n_h:(m + 1) * n_h, :] = jnp.where(re_half, bbr * pr - bbi * pi, bbr * pi + bbi * pr)
    mt_scr[SSM_ROWS:, :] = jnp.zeros((LAG_COLS - SSM_ROWS, 2 * n_p), F32)
    for t in range(SSM_CHUNK):
        pr, pi = pwr[t + 1:t + 2, :], pwi[t + 1:t + 2, :]
        cpow_scr[t * n_h:(t + 1) * n_h, :] = jnp.where(
            re_half, c_re * pr - c_im * pi, -(c_re * pi + c_im * pr)).astype(BF16)
    yield
    mt = mt_scr[...]
    lagk = _dot_nt(jnp.where(re_half, c_re, -c_im), mt, HIGHEST)
    col_h = lax.broadcasted_iota(jnp.int32, (n_h, LAG_COLS), 1)
    row_h = lax.broadcasted_iota(jnp.int32, (n_h, LAG_COLS), 0)
    lagk = lagk + jnp.where(col_h % n_h == row_h, d_ref[0], 0.0)
    for t_out in range(SSM_CHUNK):
        start = (SSM_CHUNK - 1 - t_out) * n_h
        tcol_scr[t_out * n_h:(t_out + 1) * n_h, :] = lagk[:, start:start + TOEP_TILE].astype(BF16)
    u = ut_ref[0]
    lanes = u.shape[1]
    v = _dot(mt[:SSM_ROWS, :].T.astype(BF16), u)
    xr, xi = v[:n_p], v[n_p:]
    yield
    eye = (lax.broadcasted_iota(jnp.int32, (n_p, 2 * n_p), 0)
           == lax.broadcasted_iota(jnp.int32, (n_p, 2 * n_p), 1))
    a_r = jnp.sum(jnp.where(eye, pwr[SSM_CHUNK:SSM_CHUNK + 1, :], 0.0), axis=1, keepdims=True)
    a_i = jnp.sum(jnp.where(eye, pwi[SSM_CHUNK:SSM_CHUNK + 1, :], 0.0), axis=1, keepdims=True)
    chunk = lax.broadcasted_iota(jnp.int32, (n_p, lanes), 1) % nc
    for k in range(n_steps):
        sh = 1 << k
        keep = chunk >= sh
        sr = jnp.where(keep, pltpu.roll(xr, sh, 1), 0.0)
        si = jnp.where(keep, pltpu.roll(xi, sh, 1), 0.0)
        xr, xi = xr + (a_r * sr - a_i * si), xi + (a_r * si + a_i * sr)
        a_r, a_i = a_r * a_r - a_i * a_i, 2.0 * a_r * a_i
    keep = chunk >= 1
    pr = jnp.where(keep, pltpu.roll(xr, 1, 1), 0.0)
    pi = jnp.where(keep, pltpu.roll(xi, 1, 1), 0.0)
    xprev = jnp.concatenate([pr, pi], axis=0).astype(BF16)
    yield
    for bi in range(TOEP_TILES):
        rows = slice(bi * TOEP_TILE, (bi + 1) * TOEP_TILE)
        acc = _dot(cpow_scr[rows, :], xprev)
        for bj in range(bi + 1):
            d = bi - bj
            acc = acc + _dot(tcol_scr[d * TOEP_TILE:(d + 1) * TOEP_TILE, :],
                             u[bj * TOEP_TILE:(bj + 1) * TOEP_TILE, :])
        yg_ref[0, rows, :] = jax.nn.gelu(acc).astype(BF16)


def _ssm_kernel(*refs, nc, n_steps):
    n_io = 9
    groups = [_ssm_group(*[r.at[gi:gi + 1] for r in refs[:n_io]], *[s.at[gi] for s in refs[n_io:]],
                         nc=nc, n_steps=n_steps) for gi in range(GROUPS_PER_STEP)]
    while groups:
        groups = [g for g in groups if next(g, groups) is not groups]


def _ssm_core(ut, lam2, log_dt, b_re, b_im, c_re, c_im, d_pad, nc):
    lanes = ut.shape[2]
    n_steps = max(1, (nc - 1).bit_length())
    gps = GROUPS_PER_STEP
    g3 = lambda g: (g, 0, 0)
    par = pl.BlockSpec((gps, SSM_GROUP, 2 * SSM_STATE), g3)
    return pl.pallas_call(
        functools.partial(_ssm_kernel, nc=nc, n_steps=n_steps),
        grid=(N_GROUPS // gps,),
        in_specs=[
            pl.BlockSpec((gps, SSM_ROWS, lanes), g3),
            pl.BlockSpec((gps, 2, 2 * SSM_STATE), g3),
            pl.BlockSpec((gps, 1, 1), g3),
            par, par, par, par,
            pl.BlockSpec((gps, 1, LAG_COLS), g3),
        ],
        out_specs=pl.BlockSpec((gps, SSM_ROWS, lanes), g3),
        out_shape=jax.ShapeDtypeStruct((N_GROUPS, SSM_ROWS, lanes), BF16),
        scratch_shapes=[pltpu.VMEM((gps, LAG_COLS, 2 * SSM_STATE), F32),
                        pltpu.VMEM((gps, SSM_ROWS, 2 * SSM_STATE), BF16),
                        pltpu.VMEM((gps, SSM_ROWS, TOEP_TILE), BF16)],
        compiler_params=pltpu.CompilerParams(dimension_semantics=("arbitrary",),
                                             vmem_limit_bytes=VMEM_LIMIT),
        name="ssm_core",
    )(ut, lam2, log_dt, b_re, b_im, c_re, c_im, d_pad)


def _build_bias_table(rb_ref, tab_scr):
    n_keys, width = KEY_BLOCKS * Q_BLOCK, REL_WIDTH
    s = lax.broadcasted_iota(jnp.int32, (1, width), 1)
    delta = jnp.where(s < Q_BLOCK, -s, width - s)
    rel = delta - (KEY_BLOCKS - 1) * Q_BLOCK
    idx = jnp.clip(rel, -MAX_REL, MAX_REL) + MAX_REL
    pick = (lax.broadcasted_iota(jnp.int32, (REL_PAD, width), 0) == idx).astype(F32)
    w = _dot(rb_ref[...], pick, HIGHEST)
    kk = lax.broadcasted_iota(jnp.int32, (n_keys, Q_BLOCK), 0)
    qq = lax.broadcasted_iota(jnp.int32, (n_keys, Q_BLOCK), 1)
    kc = kk // ATT_CHUNK - (KEY_BLOCKS - 1) * (Q_BLOCK // ATT_CHUNK)
    qc = qq // ATT_CHUNK
    valid = (kc <= qc) & (kc >= qc - LEFT_CHUNKS)
    for h in range(N_HEADS):
        rows = jnp.broadcast_to(w[h:h + 1, :], (n_keys, width))
        toep = pltpu.roll(rows, 0, 1, stride=1, stride_axis=0)
        tab_scr[h] = jnp.where(valid, toep[:, :Q_BLOCK] * LOG2E, MASK_VALUE)


def _attn_kernel(q_ref, k0_ref, k1_ref, k2_ref, v0_ref, v1_ref, v2_ref, rb_ref, g_ref, x_ref, wo_ref,
                 o_ref, tab_scr):
    i = pl.program_id(1)
    kv_all = ((k0_ref, v0_ref, 0), (k1_ref, v1_ref, 1), (k2_ref, v2_ref, 2))

    @pl.when((pl.program_id(0) == 0) & (i == 0))
    def _():
        _build_bias_table(rb_ref, tab_scr)

    def body(kv_list):
        q = q_ref[0]
        lane = lax.broadcasted_iota(jnp.int32, (Q_BLOCK, 2 * HEAD_DIM), 1)

        def scores(h):
            pair = slice((h // 2) * 2 * HEAD_DIM, (h // 2 + 1) * 2 * HEAD_DIM)
            qp = q[:, pair]
            qh = jnp.where((lane < HEAD_DIM) if h % 2 == 0 else (lane >= HEAD_DIM), qp, jnp.zeros_like(qp))
            return [_dot_nt(k_ref[0, :, pair], qh) for (k_ref, _, _) in kv_list]

        def probs(h, raw, half):
            cols = slice(half * HALF_Q, (half + 1) * HALF_Q)
            band_lo = half * HALF_Q
            band_hi = band_lo + (LEFT_CHUNKS + HALF_Q // ATT_CHUNK) * ATT_CHUNK
            parts = []
            for s, (_, _, rel) in zip(raw, kv_list):
                lo, hi = max(band_lo - rel * Q_BLOCK, 0), min(band_hi - rel * Q_BLOCK, Q_BLOCK)
                parts.append((lo, hi, s[lo:hi, cols] + tab_scr[h, rel * Q_BLOCK + lo:rel * Q_BLOCK + hi, cols]))
            m = parts[0][2].max(axis=0, keepdims=True)
            for _, _, s in parts[1:]:
                m = jnp.maximum(m, s.max(axis=0, keepdims=True))
            denom, out = None, []
            for lo, hi, s in parts:
                p = jnp.exp2(s - m)
                ps = p.sum(axis=0, keepdims=True)
                denom = ps if denom is None else denom + ps
                p = p.astype(BF16)
                pad = [jnp.zeros((n, HALF_Q), BF16) for n in (lo, Q_BLOCK - hi)]
                out.append(jnp.concatenate([z for z in (pad[0], p, pad[1]) if z.shape[0]], axis=0))
            return out, denom

        raw_next = scores(0)
        heads, proj, ssq = [], None, None
        for h in range(N_HEADS):
            raw = raw_next
            if h + 1 < N_HEADS:
                raw_next = scores(h + 1)
            (p_lo, d_lo), (p_hi, d_hi) = probs(h, raw, 0), probs(h, raw, 1)
            o = None
            for pl_, ph_, (_, vt_ref, _) in zip(p_lo, p_hi, kv_list):
                oj = _dot(vt_ref[h * HEAD_DIM:(h + 1) * HEAD_DIM, :], jnp.concatenate([pl_, ph_], axis=1))
                o = oj if o is None else o + oj
            heads.append(o / jnp.concatenate([d_lo, d_hi], axis=1))
            if len(heads) == HEADS_PER_PROJ:
                cols = slice((h + 1 - HEADS_PER_PROJ) * HEAD_DIM, (h + 1) * HEAD_DIM)
                ot = jnp.concatenate(heads, axis=0).T
                sq = jnp.sum(ot * ot, axis=1, keepdims=True)
                part = _dot((ot * g_ref[:, cols]).astype(BF16), wo_ref[cols, :])
                proj, ssq = (part, sq) if proj is None else (proj + part, ssq + sq)
                heads = []
        o_ref[0] = x_ref[0] + proj * lax.rsqrt(ssq / D_ATT + EPS)

    pl.when(i == 0)(lambda: body(kv_all[2:]))
    pl.when(i == 1)(lambda: body(kv_all[1:]))
    pl.when(i >= 2)(lambda: body(kv_all))


def _attention(q, k, vt, rb_pad, g_row, x, wo_att):
    nb, seq, _ = q.shape
    n_blocks = seq // Q_BLOCK
    blk = (1, Q_BLOCK, D_ATT)
    vblk = (D_ATT, Q_BLOCK)
    xblk = (1, Q_BLOCK, D_MODEL)

    def back(n):
        return lambda b, i: (b, jnp.maximum(i - n, 0), 0)

    def back_t(n):
        return lambda b, i: (0, b * n_blocks + jnp.maximum(i - n, 0))

    return pl.pallas_call(
        _attn_kernel,
        grid=(nb, seq // Q_BLOCK),
        in_specs=[
            pl.BlockSpec(blk, back(0)),
            pl.BlockSpec(blk, back(2)), pl.BlockSpec(blk, back(1)), pl.BlockSpec(blk, back(0)),
            pl.BlockSpec(vblk, back_t(2)), pl.BlockSpec(vblk, back_t(1)), pl.BlockSpec(vblk, back_t(0)),
            pl.BlockSpec((N_HEADS, REL_PAD), lambda b, i: (0, 0)),
            pl.BlockSpec((1, D_ATT), lambda b, i: (0, 0)),
            pl.BlockSpec(xblk, back(0)),
            pl.BlockSpec((D_ATT, D_MODEL), lambda b, i: (0, 0)),
        ],
        out_specs=pl.BlockSpec(xblk, back(0)),
        out_shape=jax.ShapeDtypeStruct((nb, seq, D_MODEL), F32),
        scratch_shapes=[pltpu.VMEM((N_HEADS, KEY_BLOCKS * Q_BLOCK, Q_BLOCK), F32)],
        compiler_params=pltpu.CompilerParams(dimension_semantics=("arbitrary", "arbitrary"),
                                             vmem_limit_bytes=VMEM_LIMIT),
        name="chunk_attn",
    )(q, k, k, k, vt, vt, vt, rb_pad, g_row, x, wo_att)


def _tail_kernel(yg_ref, x_hbm, wglut_ref, bglu_ref, gssm_ref, wo_ref, gffn_ref, wg_ref, wu_ref, wd_ref,
                 gfin_ref, o_hbm, xbuf, mix_scr, in_sem, out_sem, *, nc, n_tblocks, n_steps):
    step = pl.program_id(0) * n_tblocks + pl.program_id(1)
    slot = step % TAIL_SLOTS
    gather = functools.partial(_strided_copies, x_hbm, xbuf, in_sem, n_tblocks=n_tblocks)
    scatter = functools.partial(_strided_copies, o_hbm, xbuf, out_sem, n_tblocks=n_tblocks, to_hbm=True)

    @pl.when(step == 0)
    def _():
        for cp in gather(step, slot):
            cp.start()

    @pl.when(step + 1 < n_steps)
    def _():
        nxt = (step + 1) % TAIL_SLOTS

        @pl.when(step + 1 >= TAIL_SLOTS)
        def _():
            for cp in scatter(step + 1 - TAIL_SLOTS, nxt):
                cp.wait()

        for cp in gather(step + 1, nxt):
            cp.start()

    for tl in range(TB):
        yg = yg_ref[:, tl * SSM_GROUP:(tl + 1) * SSM_GROUP, :].reshape(D_SSM, nc)
        z = _dot(wglut_ref[...], yg) + bglu_ref[...]
        y2 = yg.astype(F32) * _sigmoid(z)
        ms = jnp.mean(y2 * y2, axis=0, keepdims=True)
        yn = y2 * lax.rsqrt(ms + EPS) * gssm_ref[...]
        mix_scr[tl * nc:(tl + 1) * nc, :] = yn.T.astype(BF16)

    for cp in gather(step, slot):
        cp.wait()
    t_per = FFN_ROWS // nc
    for sb in range(TB // t_per):
        rows = slice(sb * FFN_ROWS, (sb + 1) * FFN_ROWS)
        x_rows = jnp.concatenate([xbuf[slot, sb * t_per + j] for j in range(t_per)], axis=0)
        x1 = x_rows + _dot(mix_scr[rows, :], wo_ref[...])
        h2 = _rms_rows(x1, gffn_ref[...]).astype(BF16)
        x2 = x1
        for lo, hi in zip(FF_SPLIT[:-1], FF_SPLIT[1:]):
            gate = _dot(h2, wg_ref[:, lo:hi])
            up = _dot(h2, wu_ref[:, lo:hi])
            act = (gate * _sigmoid(gate) * up).astype(BF16)
            x2 = x2 + _dot(act, wd_ref[lo:hi, :])
        out = _rms_rows(x2, gfin_ref[...])
        for j in range(t_per):
            xbuf[slot, sb * t_per + j] = out[j * nc:(j + 1) * nc]

    for cp in scatter(step, slot):
        cp.start()

    @pl.when(step == n_steps - 1)
    def _():
        for back in range(min(TAIL_SLOTS, n_steps)):
            for cp in scatter(step - back, (step - back) % TAIL_SLOTS):
                cp.wait()


def _tail(yg, x4, wglut, bglu, gssm, wo_ssm, gffn, wg, wu, wd, gfin):
    nb, nc = x4.shape[0], x4.shape[1]
    n_tblocks = SSM_CHUNK // TB
    const = lambda b, i: (0, 0)

    def resident(shape):
        return pl.BlockSpec(shape, const, pipeline_mode=pl.Buffered(1))

    return pl.pallas_call(
        functools.partial(_tail_kernel, nc=nc, n_tblocks=n_tblocks, n_steps=nb * n_tblocks),
        grid=(nb, n_tblocks),
        in_specs=[
            pl.BlockSpec((N_GROUPS, TB * SSM_GROUP, nc), lambda b, i: (0, i, b)),
            pl.BlockSpec(memory_space=pl.ANY),
            resident((D_SSM, D_SSM)),
            resident((D_SSM, 1)),
            resident((D_SSM, 1)),
            resident((D_SSM, D_MODEL)),
            resident((1, D_MODEL)),
            resident((D_MODEL, D_FF)),
            resident((D_MODEL, D_FF)),
            resident((D_FF, D_MODEL)),
            resident((1, D_MODEL)),
        ],
        out_specs=pl.BlockSpec(memory_space=pl.ANY),
        out_shape=jax.ShapeDtypeStruct(x4.shape, F32),
        scratch_shapes=[pltpu.VMEM((TAIL_SLOTS, TB, nc, D_MODEL), F32),
                        pltpu.VMEM((TB * nc, D_SSM), BF16),
                        pltpu.SemaphoreType.DMA((TAIL_SLOTS,)),
                        pltpu.SemaphoreType.DMA((TAIL_SLOTS,))],
        compiler_params=pltpu.CompilerParams(dimension_semantics=("arbitrary", "arbitrary"),
                                             vmem_limit_bytes=VMEM_LIMIT),
        name="tail",
    )(yg, x4, wglut, bglu, gssm, wo_ssm, gffn, wg, wu, wd, gfin)


def _twice(a):
    return jnp.concatenate([a, a], axis=-1)


def kernel(x, norm_mix_g, w_in, ssm_lam_re, ssm_lam_im, ssm_log_dt, ssm_b_re, ssm_b_im, ssm_c_re,
           ssm_c_im, ssm_d, ssm_w_glu, ssm_b_glu, attn_rel_bias, norm_ssm_out_g, norm_att_out_g,
           w_out, norm_ffn_g, w_gate, w_up, w_down, norm_final_g):
    nb, seq, _ = x.shape
    nc = seq // SSM_CHUNK
    assert w_in.shape[0] == 1 and seq % Q_BLOCK == 0 and (nb * seq) % ROW_TILE == 0
    assert FFN_ROWS % nc == 0 and (TB * nc) % FFN_ROWS == 0
    w_in0 = w_in[0]
    scale = LOG2E / math.sqrt(HEAD_DIM)
    wqk = jnp.concatenate([w_in0[:, D_SSM:D_SSM + D_ATT] * scale, w_in0[:, D_SSM + D_ATT:D_SSM + 2 * D_ATT]],
                          axis=1).astype(BF16)
    wvt = w_in0[:, D_SSM + 2 * D_ATT:].T.astype(BF16)
    wut = w_in0[:, :D_SSM].T.astype(BF16)
    wo = w_out[0].astype(BF16)
    g_mix = norm_mix_g[0][None, :]
    x4 = x.reshape(nb, nc, SSM_CHUNK, D_MODEL)
    lam2 = _twice(jnp.stack([ssm_lam_re[0], ssm_lam_im[0]], axis=1))
    b_re2, b_im2 = _twice(ssm_b_re[0].transpose(0, 2, 1)), _twice(ssm_b_im[0].transpose(0, 2, 1))
    c_re2, c_im2 = _twice(ssm_c_re[0]), _twice(ssm_c_im[0])
    lag0 = (SSM_CHUNK - 1) * SSM_GROUP
    d_pad = jnp.pad(ssm_d[0], ((0, 0), (lag0, LAG_COLS - lag0 - SSM_GROUP)))[:, None, :]

    q, k, vt = _qkv(x.reshape(nb * seq, D_MODEL), g_mix, wqk, wvt)
    ut = _ssm_in(x4, g_mix, wut)
    yg = _ssm_core(ut, lam2, ssm_log_dt[0][:, None, None], b_re2, b_im2, c_re2, c_im2, d_pad, nc)
    x1a = _attention(q.reshape(nb, seq, D_ATT), k.reshape(nb, seq, D_ATT), vt,
                     jnp.pad(attn_rel_bias[0], ((0, 0), (0, REL_PAD - (2 * MAX_REL + 1)))),
                     norm_att_out_g[0][None, :], x, wo[D_SSM:])
    out4 = _tail(yg, x1a.reshape(nb, nc, SSM_CHUNK, D_MODEL),
                 ssm_w_glu[0].T.astype(BF16), ssm_b_glu[0][:, None], norm_ssm_out_g[0][:, None],
                 wo[:D_SSM], norm_ffn_g[0][None, :],
                 w_gate[0].astype(BF16), w_up[0].astype(BF16), w_down[0].astype(BF16),
                 norm_final_g[None, :])
    return out4.reshape(nb, seq, D_MODEL)
```

```python
import functools
import math

import jax
import jax.numpy as jnp
from jax import lax
from jax.experimental import pallas as pl
from jax.experimental.pallas import tpu as pltpu

F32 = jnp.float32
BF16 = jnp.bfloat16
HIGHEST = lax.Precision.HIGHEST

D_MODEL = 1024
D_SSM = 512
D_ATT = 512
SSM_GROUP = 16
N_GROUPS = D_SSM // SSM_GROUP
SSM_STATE = 64
HEAD_DIM = 64
N_HEADS = D_ATT // HEAD_DIM
ATT_CHUNK = 64
LEFT_CHUNKS = 8
MAX_REL = 128
D_FF = 2816
EPS = 1e-6
MASK_VALUE = -1e30
LOG2E = math.log2(math.e)

SSM_CHUNK = 64
SSM_ROWS = SSM_CHUNK * SSM_GROUP
TOEP_TILE = 256
TOEP_T = TOEP_TILE // SSM_GROUP
TOEP_TILES = SSM_ROWS // TOEP_TILE
LAG_STEPS = SSM_CHUNK + TOEP_T
LAG_COLS = LAG_STEPS * SSM_GROUP
POW_ROWS = 72
GROUPS_PER_STEP = 4
Q_BLOCK = 256
HALF_Q = 128
HEADS_PER_PROJ = 4
KEY_BLOCKS = 3
REL_WIDTH = (KEY_BLOCKS + 1) * Q_BLOCK
REL_PAD = 384
ROW_TILE = 1024
TB = 8
FF_CHUNK = 256
FF_CHUNKS = D_FF // FF_CHUNK
FFN_STAGES_PER_HEAD = (2, 2, 2, 2, 2, 2, 0, 0)
IN_SLOTS = 2
OUT_SLOTS = 2
VMEM_LIMIT = 56 * 1024 * 1024


def _sigmoid(z):
    return 1.0 / (1.0 + jnp.exp(-z))


def _rms_rows(x, g_row):
    ms = jnp.mean(x * x, axis=-1, keepdims=True)
    return x * lax.rsqrt(ms + EPS) * g_row


def _dot(a, b, precision=None):
    return jnp.dot(a, b, preferred_element_type=F32, precision=precision)


def _dot_nt(a, b, precision=None):
    return lax.dot_general(a, b, (((1,), (1,)), ((), ())), preferred_element_type=F32, precision=precision)


def _strided_copies(hbm, buf, sem, step, slot, n_tblocks, to_hbm=False):
    b, i = step // n_tblocks, step % n_tblocks
    copies = []
    for tl in range(TB):
        src, dst = hbm.at[b, :, i * TB + tl, :], buf.at[slot, tl]
        if to_hbm:
            src, dst = dst, src
        copies.append(pltpu.make_async_copy(src, dst, sem.at[slot]))
    return copies


def _qkv_kernel(x_ref, g_ref, wqk_ref, wvt_ref, q_ref, k_ref, vt_ref):
    h = _rms_rows(x_ref[...], g_ref[...]).astype(BF16)
    qk = _dot(h, wqk_ref[...])
    q_ref[...] = qk[:, 0:D_ATT].astype(BF16)
    k_ref[...] = qk[:, D_ATT:2 * D_ATT].astype(BF16)
    vt_ref[...] = _dot_nt(wvt_ref[...], h).astype(BF16)


def _qkv(x2, g_row, wqk, wvt):
    n_rows = x2.shape[0]
    const = lambda i: (0, 0)
    out_shape = jax.ShapeDtypeStruct((n_rows, D_ATT), BF16)
    out_spec = pl.BlockSpec((ROW_TILE, D_ATT), lambda i: (i, 0))
    return pl.pallas_call(
        _qkv_kernel,
        grid=(n_rows // ROW_TILE,),
        in_specs=[pl.BlockSpec((ROW_TILE, D_MODEL), lambda i: (i, 0)),
                  pl.BlockSpec((1, D_MODEL), const),
                  pl.BlockSpec((D_MODEL, 2 * D_ATT), const),
                  pl.BlockSpec((D_ATT, D_MODEL), const)],
        out_specs=[out_spec, out_spec, pl.BlockSpec((D_ATT, ROW_TILE), lambda i: (0, i))],
        out_shape=[out_shape, out_shape, jax.ShapeDtypeStruct((D_ATT, n_rows), BF16)],
        compiler_params=pltpu.CompilerParams(dimension_semantics=("arbitrary",),
                                             vmem_limit_bytes=VMEM_LIMIT),
        name="qkv",
    )(x2, g_row, wqk, wvt)


def _ssm_in_kernel(x_hbm, g_ref, wut_ref, ut_ref, xbuf, h_scr, sem, *, nc, n_tblocks, n_steps):
    step = pl.program_id(0) * n_tblocks + pl.program_id(1)
    slot = step % IN_SLOTS

    @pl.when(step == 0)
    def _():
        for cp in _strided_copies(x_hbm, xbuf, sem, step, slot, n_tblocks):
            cp.start()

    @pl.when(step + 1 < n_steps)
    def _():
        for cp in _strided_copies(x_hbm, xbuf, sem, step + 1, (step + 1) % IN_SLOTS, n_tblocks):
            cp.start()

    for cp in _strided_copies(x_hbm, xbuf, sem, step, slot, n_tblocks):
        cp.wait()
    for tl in range(TB):
        h_scr[tl * nc:(tl + 1) * nc, :] = _rms_rows(xbuf[slot, tl], g_ref[...]).astype(BF16)
    ut = _dot_nt(wut_ref[...], h_scr[...])
    for tl in range(TB):
        ut_ref[:, tl * SSM_GROUP:(tl + 1) * SSM_GROUP, :] = (
            ut[:, tl * nc:(tl + 1) * nc].reshape(N_GROUPS, SSM_GROUP, nc).astype(BF16))


def _ssm_in(x4, g_row, wut):
    nb, nc = x4.shape[0], x4.shape[1]
    n_tblocks = SSM_CHUNK // TB
    return pl.pallas_call(
        functools.partial(_ssm_in_kernel, nc=nc, n_tblocks=n_tblocks, n_steps=nb * n_tblocks),
        grid=(nb, n_tblocks),
        in_specs=[pl.BlockSpec(memory_space=pl.ANY),
                  pl.BlockSpec((1, D_MODEL), lambda b, i: (0, 0)),
                  pl.BlockSpec((D_SSM, D_MODEL), lambda b, i: (0, 0))],
        out_specs=pl.BlockSpec((N_GROUPS, TB * SSM_GROUP, nc), lambda b, i: (0, i, b)),
        out_shape=jax.ShapeDtypeStruct((N_GROUPS, SSM_ROWS, nb * nc), BF16),
        scratch_shapes=[pltpu.VMEM((IN_SLOTS, TB, nc, D_MODEL), F32),
                        pltpu.VMEM((TB * nc, D_MODEL), BF16),
                        pltpu.SemaphoreType.DMA((IN_SLOTS,))],
        compiler_params=pltpu.CompilerParams(dimension_semantics=("arbitrary", "arbitrary"),
                                             vmem_limit_bytes=VMEM_LIMIT),
        name="ssm_in",
    )(x4, g_row, wut)


def _cexp(re, im):
    mag = jnp.exp(re)
    return mag * jnp.cos(im), mag * jnp.sin(im)


def _ssm_group(ut_ref, lam_ref, ldt_ref, bre_ref, bim_ref, cre_ref, cim_ref, d_ref,
               yg_ref, mt_scr, cpow_scr, tcol_scr, *, nc, n_steps):
    n_p, n_h = SSM_STATE, SSM_GROUP
    dt = jnp.exp(ldt_ref[0])
    lam_r, lam_i = lam_ref[0, 0:1, :], lam_ref[0, 1:2, :]
    ar, ai = lam_r * dt, lam_i * dt
    lbr, lbi = _cexp(ar, ai)
    den = lam_r * lam_r + lam_i * lam_i
    fr = ((lbr - 1.0) * lam_r + lbi * lam_i) / den
    fi = (lbi * lam_r - (lbr - 1.0) * lam_i) / den
    b_re, b_im = bre_ref[0], bim_ref[0]
    bbr, bbi = fr * b_re - fi * b_im, fr * b_im + fi * b_re
    c_re, c_im = cre_ref[0], cim_ref[0]
    n = 8
    jrow = lax.broadcasted_iota(jnp.int32, (n, 1), 0).astype(F32)
    pwr, pwi = _cexp(ar * jrow, ai * jrow)
    er, ei = _cexp(ar * float(n), ai * float(n))
    while n < POW_ROWS:
        take = min(n, POW_ROWS - n)
        pwr, pwi = (jnp.concatenate([pwr, pwr[:take] * er - pwi[:take] * ei], axis=0),
                    jnp.concatenate([pwi, pwr[:take] * ei + pwi[:take] * er], axis=0))
        er, ei = er * er - ei * ei, 2.0 * er * ei
        n += take
    re_half = lax.broadcasted_iota(jnp.int32, (n_h, 2 * n_p), 1) < n_p
    for m in range(SSM_CHUNK):
        j = SSM_CHUNK - 1 - m
        pr, pi = pwr[j:j + 1, :], pwi[j:j + 1, :]
        mt_scr[m * n_h:(m + 1) * n_h, :] = jnp.where(re_half, bbr * pr - bbi * pi, bbr * pi + bbi * pr)
    mt_scr[SSM_ROWS:, :] = jnp.zeros((LAG_COLS - SSM_ROWS, 2 * n_p), F32)
    for t in range(SSM_CHUNK):
        pr, pi = pwr[t + 1:t + 2, :], pwi[t + 1:t + 2, :]
        cpow_scr[t * n_h:(t + 1) * n_h, :] = jnp.where(
            re_half, c_re * pr - c_im * pi, -(c_re * pi + c_im * pr)).astype(BF16)
    yield
    mt = mt_scr[...]
    lagk = _dot_nt(jnp.where(re_half, c_re, -c_im), mt, HIGHEST)
    col_h = lax.broadcasted_iota(jnp.int32, (n_h, LAG_COLS), 1)
    row_h = lax.broadcasted_iota(jnp.int32, (n_h, LAG_COLS), 0)
    lagk = lagk + jnp.where(col_h % n_h == row_h, d_ref[0], 0.0)
    for t_out in range(SSM_CHUNK):
        start = (SSM_CHUNK - 1 - t_out) * n_h
        tcol_scr[t_out * n_h:(t_out + 1) * n_h, :] = lagk[:, start:start + TOEP_TILE].astype(BF16)
    u = ut_ref[0]
    lanes = u.shape[1]
    v = _dot(mt[:SSM_ROWS, :].T.astype(BF16), u)
    xr, xi = v[:n_p], v[n_p:]
    yield
    eye = (lax.broadcasted_iota(jnp.int32, (n_p, 2 * n_p), 0)
           == lax.broadcasted_iota(jnp.int32, (n_p, 2 * n_p), 1))
    a_r = jnp.sum(jnp.where(eye, pwr[SSM_CHUNK:SSM_CHUNK + 1, :], 0.0), axis=1, keepdims=True)
    a_i = jnp.sum(jnp.where(eye, pwi[SSM_CHUNK:SSM_CHUNK + 1, :], 0.0), axis=1, keepdims=True)
    chunk = lax.broadcasted_iota(jnp.int32, (n_p, lanes), 1) % nc
    for k in range(n_steps):
        sh = 1 << k
        keep = chunk >= sh
        sr = jnp.where(keep, pltpu.roll(xr, sh, 1), 0.0)
        si = jnp.where(keep, pltpu.roll(xi, sh, 1), 0.0)
        xr, xi = xr + (a_r * sr - a_i * si), xi + (a_r * si + a_i * sr)
        a_r, a_i = a_r * a_r - a_i * a_i, 2.0 * a_r * a_i
    keep = chunk >= 1
    pr = jnp.where(keep, pltpu.roll(xr, 1, 1), 0.0)
    pi = jnp.where(keep, pltpu.roll(xi, 1, 1), 0.0)
    xprev = jnp.concatenate([pr, pi], axis=0).astype(BF16)
    yield
    for bi in range(TOEP_TILES):
        rows = slice(bi * TOEP_TILE, (bi + 1) * TOEP_TILE)
        acc = _dot(cpow_scr[rows, :], xprev)
        for bj in range(bi + 1):
            d = bi - bj
            acc = acc + _dot(tcol_scr[d * TOEP_TILE:(d + 1) * TOEP_TILE, :],
                             u[bj * TOEP_TILE:(bj + 1) * TOEP_TILE, :])
        yg_ref[0, rows, :] = jax.nn.gelu(acc).astype(BF16)


def _ssm_kernel(*refs, nc, n_steps):
    n_io = 9
    groups = [_ssm_group(*[r.at[gi:gi + 1] for r in refs[:n_io]], *[s.at[gi] for s in refs[n_io:]],
                         nc=nc, n_steps=n_steps) for gi in range(GROUPS_PER_STEP)]
    while groups:
        groups = [g for g in groups if next(g, groups) is not groups]


def _ssm_core(ut, lam2, log_dt, b_re, b_im, c_re, c_im, d_pad, nc):
    lanes = ut.shape[2]
    n_steps = max(1, (nc - 1).bit_length())
    gps = GROUPS_PER_STEP
    g3 = lambda g: (g, 0, 0)
    par = pl.BlockSpec((gps, SSM_GROUP, 2 * SSM_STATE), g3)
    return pl.pallas_call(
        functools.partial(_ssm_kernel, nc=nc, n_steps=n_steps),
        grid=(N_GROUPS // gps,),
        in_specs=[
            pl.BlockSpec((gps, SSM_ROWS, lanes), g3),
            pl.BlockSpec((gps, 2, 2 * SSM_STATE), g3),
            pl.BlockSpec((gps, 1, 1), g3),
            par, par, par, par,
            pl.BlockSpec((gps, 1, LAG_COLS), g3),
        ],
        out_specs=pl.BlockSpec((gps, SSM_ROWS, lanes), g3),
        out_shape=jax.ShapeDtypeStruct((N_GROUPS, SSM_ROWS, lanes), BF16),
        scratch_shapes=[pltpu.VMEM((gps, LAG_COLS, 2 * SSM_STATE), F32),
                        pltpu.VMEM((gps, SSM_ROWS, 2 * SSM_STATE), BF16),
                        pltpu.VMEM((gps, SSM_ROWS, TOEP_TILE), BF16)],
        compiler_params=pltpu.CompilerParams(dimension_semantics=("arbitrary",),
                                             vmem_limit_bytes=VMEM_LIMIT),
        name="ssm_core",
    )(ut, lam2, log_dt, b_re, b_im, c_re, c_im, d_pad)


def _ssm_out_kernel(yg_ref, wglut_ref, bglu_ref, gssm_ref, o_hbm, ybuf, sem, *, nc, n_tblocks, n_steps):
    step = pl.program_id(0) * n_tblocks + pl.program_id(1)
    slot = step % OUT_SLOTS
    scatter = functools.partial(_strided_copies, o_hbm, ybuf, sem, n_tblocks=n_tblocks, to_hbm=True)

    @pl.when(step >= OUT_SLOTS)
    def _():
        for cp in scatter(step - OUT_SLOTS, slot):
            cp.wait()

    for tl in range(TB):
        yg = yg_ref[:, tl * SSM_GROUP:(tl + 1) * SSM_GROUP, :].reshape(D_SSM, nc)
        z = _dot(wglut_ref[...], yg) + bglu_ref[...]
        y2 = yg.astype(F32) * _sigmoid(z)
        ms = jnp.mean(y2 * y2, axis=0, keepdims=True)
        ybuf[slot, tl] = (y2 * lax.rsqrt(ms + EPS) * gssm_ref[...]).T
    for cp in scatter(step, slot):
        cp.start()

    @pl.when(step == n_steps - 1)
    def _():
        for back in range(min(OUT_SLOTS, n_steps)):
            for cp in scatter(step - back, (step - back) % OUT_SLOTS):
                cp.wait()


def _ssm_out(yg, wglut, bglu, gssm, nb, nc):
    n_tblocks = SSM_CHUNK // TB
    const = lambda b, i: (0, 0)
    return pl.pallas_call(
        functools.partial(_ssm_out_kernel, nc=nc, n_tblocks=n_tblocks, n_steps=nb * n_tblocks),
        grid=(nb, n_tblocks),
        in_specs=[pl.BlockSpec((N_GROUPS, TB * SSM_GROUP, nc), lambda b, i: (0, i, b)),
                  pl.BlockSpec((D_SSM, D_SSM), const),
                  pl.BlockSpec((D_SSM, 1), const),
                  pl.BlockSpec((D_SSM, 1), const)],
        out_specs=pl.BlockSpec(memory_space=pl.ANY),
        out_shape=jax.ShapeDtypeStruct((nb, nc, SSM_CHUNK, D_SSM), F32),
        scratch_shapes=[pltpu.VMEM((OUT_SLOTS, TB, nc, D_SSM), F32),
                        pltpu.SemaphoreType.DMA((OUT_SLOTS,))],
        compiler_params=pltpu.CompilerParams(dimension_semantics=("arbitrary", "arbitrary"),
                                             vmem_limit_bytes=VMEM_LIMIT),
        name="ssm_out",
    )(yg, wglut, bglu, gssm)


def _build_bias_table(rb_ref, tab_scr):
    n_keys, width = KEY_BLOCKS * Q_BLOCK, REL_WIDTH
    s = lax.broadcasted_iota(jnp.int32, (1, width), 1)
    delta = jnp.where(s < Q_BLOCK, -s, width - s)
    rel = delta - (KEY_BLOCKS - 1) * Q_BLOCK
    idx = jnp.clip(rel, -MAX_REL, MAX_REL) + MAX_REL
    pick = (lax.broadcasted_iota(jnp.int32, (REL_PAD, width), 0) == idx).astype(F32)
    w = _dot(rb_ref[...], pick, HIGHEST)
    kk = lax.broadcasted_iota(jnp.int32, (n_keys, Q_BLOCK), 0)
    qq = lax.broadcasted_iota(jnp.int32, (n_keys, Q_BLOCK), 1)
    kc = kk // ATT_CHUNK - (KEY_BLOCKS - 1) * (Q_BLOCK // ATT_CHUNK)
    qc = qq // ATT_CHUNK
    valid = (kc <= qc) & (kc >= qc - LEFT_CHUNKS)
    for h in range(N_HEADS):
        rows = jnp.broadcast_to(w[h:h + 1, :], (n_keys, width))
        toep = pltpu.roll(rows, 0, 1, stride=1, stride_axis=0)
        tab_scr[h] = jnp.where(valid, toep[:, :Q_BLOCK] * LOG2E, MASK_VALUE)


def _ffn_stages(x1, gffn_ref, wg_ref, wu_ref, wd_ref):
    h2 = _rms_rows(x1, gffn_ref[...]).astype(BF16)
    acc, pending = None, None
    for c in range(FF_CHUNKS + 1):
        issued = None
        if c < FF_CHUNKS:
            cols = slice(c * FF_CHUNK, (c + 1) * FF_CHUNK)
            issued = (cols, _dot(h2, wg_ref[:, cols]), _dot(h2, wu_ref[:, cols]))
        if pending is not None:
            cols, gate, up = pending
            part = _dot((gate * _sigmoid(gate) * up).astype(BF16), wd_ref[cols, :])
            acc = part if acc is None else acc + part
        pending = issued
        yield acc


def _mix_ffn_kernel(q_ref, k0_ref, k1_ref, k2_ref, v0_ref, v1_ref, v2_ref, rb_ref, g_ref, x_ref, ys_ref, wo_ref,
                    gffn_ref, wg_ref, wu_ref, wd_ref, gfin_ref, o_ref, tab_scr, x1_scr, *, n_blocks):
    i = pl.program_id(1)
    cur = i % 2
    kv_all = ((k0_ref, v0_ref, 0), (k1_ref, v1_ref, 1), (k2_ref, v2_ref, 2))

    @pl.when((pl.program_id(0) == 0) & (i == 0))
    def _():
        _build_bias_table(rb_ref, tab_scr)

    def body(kv_list, with_ffn):
        ffn = _ffn_stages(x1_scr[1 - cur], gffn_ref, wg_ref, wu_ref, wd_ref) if with_ffn else None
        acc = None
        if not kv_list:
            for acc in ffn:
                pass
            o_ref[0] = _rms_rows(x1_scr[1 - cur] + acc, gfin_ref[...])
            return
        q = q_ref[0]
        lane = lax.broadcasted_iota(jnp.int32, (Q_BLOCK, 2 * HEAD_DIM), 1)
        ssm_proj = _dot(ys_ref[0].astype(BF16), wo_ref[0:D_SSM, :])

        def scores(h):
            pair = slice((h // 2) * 2 * HEAD_DIM, (h // 2 + 1) * 2 * HEAD_DIM)
            qp = q[:, pair]
            qh = jnp.where((lane < HEAD_DIM) if h % 2 == 0 else (lane >= HEAD_DIM), qp, jnp.zeros_like(qp))
            return [_dot_nt(k_ref[0, :, pair], qh) for (k_ref, _, _) in kv_list]

        def probs(h, raw, half):
            cols = slice(half * HALF_Q, (half + 1) * HALF_Q)
            band_lo = half * HALF_Q
            band_hi = band_lo + (LEFT_CHUNKS + HALF_Q // ATT_CHUNK) * ATT_CHUNK
            parts = []
            for s, (_, _, rel) in zip(raw, kv_list):
                lo, hi = max(band_lo - rel * Q_BLOCK, 0), min(band_hi - rel * Q_BLOCK, Q_BLOCK)
                parts.append((lo, hi, s[lo:hi, cols] + tab_scr[h, rel * Q_BLOCK + lo:rel * Q_BLOCK + hi, cols]))
            m = parts[0][2].max(axis=0, keepdims=True)
            for _, _, s in parts[1:]:
                m = jnp.maximum(m, s.max(axis=0, keepdims=True))
            denom, out = None, []
            for lo, hi, s in parts:
                p = jnp.exp2(s - m)
                ps = p.sum(axis=0, keepdims=True)
                denom = ps if denom is None else denom + ps
                p = p.astype(BF16)
                pad = [jnp.zeros((n, HALF_Q), BF16) for n in (lo, Q_BLOCK - hi)]
                out.append(jnp.concatenate([z for z in (pad[0], p, pad[1]) if z.shape[0]], axis=0))
            return out, denom

        raw_next = scores(0)
        heads, proj, ssq = [], None, None
        for h in range(N_HEADS):
            raw = raw_next
            if h + 1 < N_HEADS:
                raw_next = scores(h + 1)
            if with_ffn and FFN_STAGES_PER_HEAD[h]:
                for _ in range(FFN_STAGES_PER_HEAD[h]):
                    acc = next(ffn)
                if sum(FFN_STAGES_PER_HEAD[:h + 1]) == FF_CHUNKS + 1:
                    o_ref[0] = _rms_rows(x1_scr[1 - cur] + acc, gfin_ref[...])
            (p_lo, d_lo), (p_hi, d_hi) = probs(h, raw, 0), probs(h, raw, 1)
            o = None
            for pl_, ph_, (_, vt_ref, _) in zip(p_lo, p_hi, kv_list):
                oj = _dot(vt_ref[h * HEAD_DIM:(h + 1) * HEAD_DIM, :], jnp.concatenate([pl_, ph_], axis=1))
                o = oj if o is None else o + oj
            heads.append(o / jnp.concatenate([d_lo, d_hi], axis=1))
            if len(heads) == HEADS_PER_PROJ:
                cols = slice((h + 1 - HEADS_PER_PROJ) * HEAD_DIM, (h + 1) * HEAD_DIM)
                ot = jnp.concatenate(heads, axis=0).T
                sq = jnp.sum(ot * ot, axis=1, keepdims=True)
                part = _dot((ot * g_ref[:, cols]).astype(BF16),
                            wo_ref[D_SSM + cols.start:D_SSM + cols.stop, :])
                proj, ssq = (part, sq) if proj is None else (proj + part, ssq + sq)
                heads = []
        x1_scr[cur] = x_ref[0] + ssm_proj + proj * lax.rsqrt(ssq / D_ATT + EPS)

    pl.when(i == 0)(lambda: body(kv_all[2:], False))
    pl.when(i == 1)(lambda: body(kv_all[1:], True))
    pl.when((i >= 2) & (i < n_blocks))(lambda: body(kv_all, True))
    pl.when(i == n_blocks)(lambda: body((), True))


def _mix_ffn(q, k, vt, rb_pad, g_row, x, ysn, wo, gffn, wg, wu, wd, gfin):
    nb, seq, _ = q.shape
    n_blocks = seq // Q_BLOCK
    blk = (1, Q_BLOCK, D_ATT)
    vblk = (D_ATT, Q_BLOCK)
    xblk = (1, Q_BLOCK, D_MODEL)
    const = lambda b, i: (0, 0)

    def resident(shape):
        return pl.BlockSpec(shape, const, pipeline_mode=pl.Buffered(1))

    def back(n):
        return lambda b, i: (b, jnp.maximum(jnp.minimum(i, n_blocks - 1) - n, 0), 0)

    def back_t(n):
        return lambda b, i: (0, b * n_blocks + jnp.maximum(jnp.minimum(i, n_blocks - 1) - n, 0))

    return pl.pallas_call(
        functools.partial(_mix_ffn_kernel, n_blocks=n_blocks),
        grid=(nb, n_blocks + 1),
        in_specs=[
            pl.BlockSpec(blk, back(0)),
            pl.BlockSpec(blk, back(2)), pl.BlockSpec(blk, back(1)), pl.BlockSpec(blk, back(0)),
            pl.BlockSpec(vblk, back_t(2)), pl.BlockSpec(vblk, back_t(1)), pl.BlockSpec(vblk, back_t(0)),
            pl.BlockSpec((N_HEADS, REL_PAD), const),
            pl.BlockSpec((1, D_ATT), const),
            pl.BlockSpec(xblk, back(0)),
            pl.BlockSpec(blk, back(0)),
            resident((D_MODEL, D_MODEL)),
            pl.BlockSpec((1, D_MODEL), const),
            resident((D_MODEL, D_FF)),
            resident((D_MODEL, D_FF)),
            resident((D_FF, D_MODEL)),
            pl.BlockSpec((1, D_MODEL), const),
        ],
        out_specs=pl.BlockSpec(xblk, lambda b, i: (b, jnp.maximum(i - 1, 0), 0)),
        out_shape=jax.ShapeDtypeStruct((nb, seq, D_MODEL), F32),
        scratch_shapes=[pltpu.VMEM((N_HEADS, KEY_BLOCKS * Q_BLOCK, Q_BLOCK), F32),
                        pltpu.VMEM((2, Q_BLOCK, D_MODEL), F32)],
        compiler_params=pltpu.CompilerParams(dimension_semantics=("arbitrary", "arbitrary"),
                                             vmem_limit_bytes=VMEM_LIMIT),
        name="mix_ffn",
    )(q, k, k, k, vt, vt, vt, rb_pad, g_row, x, ysn, wo, gffn, wg, wu, wd, gfin)


def _twice(a):
    return jnp.concatenate([a, a], axis=-1)


def kernel(x, norm_mix_g, w_in, ssm_lam_re, ssm_lam_im, ssm_log_dt, ssm_b_re, ssm_b_im, ssm_c_re,
           ssm_c_im, ssm_d, ssm_w_glu, ssm_b_glu, attn_rel_bias, norm_ssm_out_g, norm_att_out_g,
           w_out, norm_ffn_g, w_gate, w_up, w_down, norm_final_g):
    nb, seq, _ = x.shape
    nc = seq // SSM_CHUNK
    assert w_in.shape[0] == 1 and seq % Q_BLOCK == 0 and (nb * seq) % ROW_TILE == 0
    w_in0 = w_in[0]
    scale = LOG2E / math.sqrt(HEAD_DIM)
    wqk = jnp.concatenate([w_in0[:, D_SSM:D_SSM + D_ATT] * scale, w_in0[:, D_SSM + D_ATT:D_SSM + 2 * D_ATT]],
                          axis=1).astype(BF16)
    wvt = w_in0[:, D_SSM + 2 * D_ATT:].T.astype(BF16)
    wut = w_in0[:, :D_SSM].T.astype(BF16)
    g_mix = norm_mix_g[0][None, :]
    x4 = x.reshape(nb, nc, SSM_CHUNK, D_MODEL)
    lam2 = _twice(jnp.stack([ssm_lam_re[0], ssm_lam_im[0]], axis=1))
    b_re2, b_im2 = _twice(ssm_b_re[0].transpose(0, 2, 1)), _twice(ssm_b_im[0].transpose(0, 2, 1))
    c_re2, c_im2 = _twice(ssm_c_re[0]), _twice(ssm_c_im[0])
    lag0 = (SSM_CHUNK - 1) * SSM_GROUP
    d_pad = jnp.pad(ssm_d[0], ((0, 0), (lag0, LAG_COLS - lag0 - SSM_GROUP)))[:, None, :]

    q, k, vt = _qkv(x.reshape(nb * seq, D_MODEL), g_mix, wqk, wvt)
    ut = _ssm_in(x4, g_mix, wut)
    yg = _ssm_core(ut, lam2, ssm_log_dt[0][:, None, None], b_re2, b_im2, c_re2, c_im2, d_pad, nc)
    ysn = _ssm_out(yg, ssm_w_glu[0].T.astype(BF16), ssm_b_glu[0][:, None], norm_ssm_out_g[0][:, None], nb, nc)
    return _mix_ffn(q.reshape(nb, seq, D_ATT), k.reshape(nb, seq, D_ATT), vt,
                    jnp.pad(attn_rel_bias[0], ((0, 0), (0, REL_PAD - (2 * MAX_REL + 1)))),
                    norm_att_out_g[0][None, :], x, ysn.reshape(nb, seq, D_SSM), w_out[0].astype(BF16),
                    norm_ffn_g[0][None, :], w_gate[0].astype(BF16), w_up[0].astype(BF16),
                    w_down[0].astype(BF16), norm_final_g[None, :])
```

```python
import functools
import math

import jax
import jax.numpy as jnp
from jax import lax
from jax.experimental import pallas as pl
from jax.experimental.pallas import tpu as pltpu

F32 = jnp.float32
BF16 = jnp.bfloat16
HIGHEST = lax.Precision.HIGHEST

D_MODEL = 1024
D_SSM = 512
D_ATT = 512
SSM_GROUP = 16
N_GROUPS = D_SSM // SSM_GROUP
SSM_STATE = 64
HEAD_DIM = 64
N_HEADS = D_ATT // HEAD_DIM
ATT_CHUNK = 64
LEFT_CHUNKS = 8
MAX_REL = 128
D_FF = 2816
EPS = 1e-6
MASK_VALUE = -1e30
LOG2E = math.log2(math.e)

SSM_CHUNK = 64
SSM_ROWS = SSM_CHUNK * SSM_GROUP
TOEP_TILE = 256
TOEP_T = TOEP_TILE // SSM_GROUP
TOEP_TILES = SSM_ROWS // TOEP_TILE
LAG_STEPS = SSM_CHUNK + TOEP_T
LAG_COLS = LAG_STEPS * SSM_GROUP
POW_ROWS = 72
GROUPS_PER_STEP = 4
Q_BLOCK = 256
HALF_Q = 128
HEADS_PER_PROJ = 4
KEY_BLOCKS = 3
REL_WIDTH = (KEY_BLOCKS + 1) * Q_BLOCK
REL_PAD = 384
ROW_TILE = 1024
TB = 8
FF_CHUNK = 256
FF_CHUNKS = D_FF // FF_CHUNK
FFN_STAGES_PER_HEAD = (2, 2, 2, 2, 1, 1, 1, 1)
X1_SLOTS = 3
IN_SLOTS = 2
OUT_SLOTS = 2
VMEM_LIMIT = 56 * 1024 * 1024


def _sigmoid(z):
    return 1.0 / (1.0 + jnp.exp(-z))


def _rms_rows(x, g_row):
    ms = jnp.mean(x * x, axis=-1, keepdims=True)
    return x * lax.rsqrt(ms + EPS) * g_row


def _dot(a, b, precision=None):
    return jnp.dot(a, b, preferred_element_type=F32, precision=precision)


def _dot_nt(a, b, precision=None):
    return lax.dot_general(a, b, (((1,), (1,)), ((), ())), preferred_element_type=F32, precision=precision)


def _strided_copies(hbm, buf, sem, step, slot, n_tblocks, to_hbm=False):
    b, i = step // n_tblocks, step % n_tblocks
    copies = []
    for tl in range(TB):
        src, dst = hbm.at[b, :, i * TB + tl, :], buf.at[slot, tl]
        if to_hbm:
            src, dst = dst, src
        copies.append(pltpu.make_async_copy(src, dst, sem.at[slot]))
    return copies


def _qkv_kernel(x_ref, g_ref, wqk_ref, wvt_ref, q_ref, k_ref, vt_ref):
    h = _rms_rows(x_ref[...], g_ref[...]).astype(BF16)
    qk = _dot(h, wqk_ref[...])
    q_ref[...] = qk[:, 0:D_ATT].astype(BF16)
    k_ref[...] = qk[:, D_ATT:2 * D_ATT].astype(BF16)
    vt_ref[...] = _dot_nt(wvt_ref[...], h).astype(BF16)


def _qkv(x2, g_row, wqk, wvt):
    n_rows = x2.shape[0]
    const = lambda i: (0, 0)
    out_shape = jax.ShapeDtypeStruct((n_rows, D_ATT), BF16)
    out_spec = pl.BlockSpec((ROW_TILE, D_ATT), lambda i: (i, 0))
    return pl.pallas_call(
        _qkv_kernel,
        grid=(n_rows // ROW_TILE,),
        in_specs=[pl.BlockSpec((ROW_TILE, D_MODEL), lambda i: (i, 0)),
                  pl.BlockSpec((1, D_MODEL), const),
                  pl.BlockSpec((D_MODEL, 2 * D_ATT), const),
                  pl.BlockSpec((D_ATT, D_MODEL), const)],
        out_specs=[out_spec, out_spec, pl.BlockSpec((D_ATT, ROW_TILE), lambda i: (0, i))],
        out_shape=[out_shape, out_shape, jax.ShapeDtypeStruct((D_ATT, n_rows), BF16)],
        compiler_params=pltpu.CompilerParams(dimension_semantics=("arbitrary",),
                                             vmem_limit_bytes=VMEM_LIMIT),
        name="qkv",
    )(x2, g_row, wqk, wvt)


def _ssm_in_kernel(x_hbm, g_ref, wut_ref, ut_ref, xbuf, h_scr, sem, *, nc, n_tblocks, n_steps):
    step = pl.program_id(0) * n_tblocks + pl.program_id(1)
    slot = step % IN_SLOTS

    @pl.when(step == 0)
    def _():
        for cp in _strided_copies(x_hbm, xbuf, sem, step, slot, n_tblocks):
            cp.start()

    @pl.when(step + 1 < n_steps)
    def _():
        for cp in _strided_copies(x_hbm, xbuf, sem, step + 1, (step + 1) % IN_SLOTS, n_tblocks):
            cp.start()

    for cp in _strided_copies(x_hbm, xbuf, sem, step, slot, n_tblocks):
        cp.wait()
    for tl in range(TB):
        h_scr[tl * nc:(tl + 1) * nc, :] = _rms_rows(xbuf[slot, tl], g_ref[...]).astype(BF16)
    ut = _dot_nt(wut_ref[...], h_scr[...])
    for tl in range(TB):
        ut_ref[:, tl * SSM_GROUP:(tl + 1) * SSM_GROUP, :] = (
            ut[:, tl * nc:(tl + 1) * nc].reshape(N_GROUPS, SSM_GROUP, nc).astype(BF16))


def _ssm_in(x4, g_row, wut):
    nb, nc = x4.shape[0], x4.shape[1]
    n_tblocks = SSM_CHUNK // TB
    return pl.pallas_call(
        functools.partial(_ssm_in_kernel, nc=nc, n_tblocks=n_tblocks, n_steps=nb * n_tblocks),
        grid=(nb, n_tblocks),
        in_specs=[pl.BlockSpec(memory_space=pl.ANY),
                  pl.BlockSpec((1, D_MODEL), lambda b, i: (0, 0)),
                  pl.BlockSpec((D_SSM, D_MODEL), lambda b, i: (0, 0))],
        out_specs=pl.BlockSpec((N_GROUPS, TB * SSM_GROUP, nc), lambda b, i: (0, i, b)),
        out_shape=jax.ShapeDtypeStruct((N_GROUPS, SSM_ROWS, nb * nc), BF16),
        scratch_shapes=[pltpu.VMEM((IN_SLOTS, TB, nc, D_MODEL), F32),
                        pltpu.VMEM((TB * nc, D_MODEL), BF16),
                        pltpu.SemaphoreType.DMA((IN_SLOTS,))],
        compiler_params=pltpu.CompilerParams(dimension_semantics=("arbitrary", "arbitrary"),
                                             vmem_limit_bytes=VMEM_LIMIT),
        name="ssm_in",
    )(x4, g_row, wut)


def _cexp(re, im):
    mag = jnp.exp(re)
    return mag * jnp.cos(im), mag * jnp.sin(im)


def _ssm_group(ut_ref, lam_ref, ldt_ref, bre_ref, bim_ref, cre_ref, cim_ref, d_ref,
               yg_ref, mt_scr, cpow_scr, tcol_scr, *, nc, n_steps):
    n_p, n_h = SSM_STATE, SSM_GROUP
    dt = jnp.exp(ldt_ref[0])
    lam_r, lam_i = lam_ref[0, 0:1, :], lam_ref[0, 1:2, :]
    ar, ai = lam_r * dt, lam_i * dt
    lbr, lbi = _cexp(ar, ai)
    den = lam_r * lam_r + lam_i * lam_i
    fr = ((lbr - 1.0) * lam_r + lbi * lam_i) / den
    fi = (lbi * lam_r - (lbr - 1.0) * lam_i) / den
    b_re, b_im = bre_ref[0], bim_ref[0]
    bbr, bbi = fr * b_re - fi * b_im, fr * b_im + fi * b_re
    c_re, c_im = cre_ref[0], cim_ref[0]
    n = 8
    jrow = lax.broadcasted_iota(jnp.int32, (n, 1), 0).astype(F32)
    pwr, pwi = _cexp(ar * jrow, ai * jrow)
    er, ei = _cexp(ar * float(n), ai * float(n))
    while n < POW_ROWS:
        take = min(n, POW_ROWS - n)
        pwr, pwi = (jnp.concatenate([pwr, pwr[:take] * er - pwi[:take] * ei], axis=0),
                    jnp.concatenate([pwi, pwr[:take] * ei + pwi[:take] * er], axis=0))
        er, ei = er * er - ei * ei, 2.0 * er * ei
        n += take
    re_half = lax.broadcasted_iota(jnp.int32, (n_h, 2 * n_p), 1) < n_p
    for m in range(SSM_CHUNK):
        j = SSM_CHUNK - 1 - m
        pr, pi = pwr[j:j + 1, :], pwi[j:j + 1, :]
        mt_scr[m * n_h:(m + 1) * n_h, :] = jnp.where(re_half, bbr * pr - bbi * pi, bbr * pi + bbi * pr)
    mt_scr[SSM_ROWS:, :] = jnp.zeros((LAG_COLS - SSM_ROWS, 2 * n_p), F32)
    for t in range(SSM_CHUNK):
        pr, pi = pwr[t + 1:t + 2, :], pwi[t + 1:t + 2, :]
        cpow_scr[t * n_h:(t + 1) * n_h, :] = jnp.where(
            re_half, c_re * pr - c_im * pi, -(c_re * pi + c_im * pr)).astype(BF16)
    yield
    mt = mt_scr[...]
    lagk = _dot_nt(jnp.where(re_half, c_re, -c_im), mt, HIGHEST)
    col_h = lax.broadcasted_iota(jnp.int32, (n_h, LAG_COLS), 1)
    row_h = lax.broadcasted_iota(jnp.int32, (n_h, LAG_COLS), 0)
    lagk = lagk + jnp.where(col_h % n_h == row_h, d_ref[0], 0.0)
    for t_out in range(SSM_CHUNK):
        start = (SSM_CHUNK - 1 - t_out) * n_h
        tcol_scr[t_out * n_h:(t_out + 1) * n_h, :] = lagk[:, start:start + TOEP_TILE].astype(BF16)
    u = ut_ref[0]
    lanes = u.shape[1]
    v = _dot(mt[:SSM_ROWS, :].T.astype(BF16), u)
    xr, xi = v[:n_p], v[n_p:]
    yield
    eye = (lax.broadcasted_iota(jnp.int32, (n_p, 2 * n_p), 0)
           == lax.broadcasted_iota(jnp.int32, (n_p, 2 * n_p), 1))
    a_r = jnp.sum(jnp.where(eye, pwr[SSM_CHUNK:SSM_CHUNK + 1, :], 0.0), axis=1, keepdims=True)
    a_i = jnp.sum(jnp.where(eye, pwi[SSM_CHUNK:SSM_CHUNK + 1, :], 0.0), axis=1, keepdims=True)
    chunk = lax.broadcasted_iota(jnp.int32, (n_p, lanes), 1) % nc
    for k in range(n_steps):
        sh = 1 << k
        keep = chunk >= sh
        sr = jnp.where(keep, pltpu.roll(xr, sh, 1), 0.0)
        si = jnp.where(keep, pltpu.roll(xi, sh, 1), 0.0)
        xr, xi = xr + (a_r * sr - a_i * si), xi + (a_r * si + a_i * sr)
        a_r, a_i = a_r * a_r - a_i * a_i, 2.0 * a_r * a_i
    keep = chunk >= 1
    pr = jnp.where(keep, pltpu.roll(xr, 1, 1), 0.0)
    pi = jnp.where(keep, pltpu.roll(xi, 1, 1), 0.0)
    xprev = jnp.concatenate([pr, pi], axis=0).astype(BF16)
    yield
    for bi in range(TOEP_TILES):
        rows = slice(bi * TOEP_TILE, (bi + 1) * TOEP_TILE)
        acc = _dot(cpow_scr[rows, :], xprev)
        for bj in range(bi + 1):
            d = bi - bj
            acc = acc + _dot(tcol_scr[d * TOEP_TILE:(d + 1) * TOEP_TILE, :],
                             u[bj * TOEP_TILE:(bj + 1) * TOEP_TILE, :])
        yg_ref[0, rows, :] = jax.nn.gelu(acc).astype(BF16)


def _ssm_kernel(*refs, nc, n_steps):
    n_io = 9
    groups = [_ssm_group(*[r.at[gi:gi + 1] for r in refs[:n_io]], *[s.at[gi] for s in refs[n_io:]],
                         nc=nc, n_steps=n_steps) for gi in range(GROUPS_PER_STEP)]
    while groups:
        groups = [g for g in groups if next(g, groups) is not groups]


def _ssm_core(ut, lam2, log_dt, b_re, b_im, c_re, c_im, d_pad, nc):
    lanes = ut.shape[2]
    n_steps = max(1, (nc - 1).bit_length())
    gps = GROUPS_PER_STEP
    g3 = lambda g: (g, 0, 0)
    par = pl.BlockSpec((gps, SSM_GROUP, 2 * SSM_STATE), g3)
    return pl.pallas_call(
        functools.partial(_ssm_kernel, nc=nc, n_steps=n_steps),
        grid=(N_GROUPS // gps,),
        in_specs=[
            pl.BlockSpec((gps, SSM_ROWS, lanes), g3),
            pl.BlockSpec((gps, 2, 2 * SSM_STATE), g3),
            pl.BlockSpec((gps, 1, 1), g3),
            par, par, par, par,
            pl.BlockSpec((gps, 1, LAG_COLS), g3),
        ],
        out_specs=pl.BlockSpec((gps, SSM_ROWS, lanes), g3),
        out_shape=jax.ShapeDtypeStruct((N_GROUPS, SSM_ROWS, lanes), BF16),
        scratch_shapes=[pltpu.VMEM((gps, LAG_COLS, 2 * SSM_STATE), F32),
                        pltpu.VMEM((gps, SSM_ROWS, 2 * SSM_STATE), BF16),
                        pltpu.VMEM((gps, SSM_ROWS, TOEP_TILE), BF16)],
        compiler_params=pltpu.CompilerParams(dimension_semantics=("arbitrary",),
                                             vmem_limit_bytes=VMEM_LIMIT),
        name="ssm_core",
    )(ut, lam2, log_dt, b_re, b_im, c_re, c_im, d_pad)


def _ssm_out_kernel(yg_ref, wglut_ref, bglu_ref, gssm_ref, o_hbm, ybuf, sem, *, nc, n_tblocks, n_steps):
    step = pl.program_id(0) * n_tblocks + pl.program_id(1)
    slot = step % OUT_SLOTS
    scatter = functools.partial(_strided_copies, o_hbm, ybuf, sem, n_tblocks=n_tblocks, to_hbm=True)

    @pl.when(step >= OUT_SLOTS)
    def _():
        for cp in scatter(step - OUT_SLOTS, slot):
            cp.wait()

    for tl in range(TB):
        yg = yg_ref[:, tl * SSM_GROUP:(tl + 1) * SSM_GROUP, :].reshape(D_SSM, nc)
        z = _dot(wglut_ref[...], yg) + bglu_ref[...]
        y2 = yg.astype(F32) * _sigmoid(z)
        ms = jnp.mean(y2 * y2, axis=0, keepdims=True)
        ybuf[slot, tl] = (y2 * lax.rsqrt(ms + EPS) * gssm_ref[...]).T
    for cp in scatter(step, slot):
        cp.start()

    @pl.when(step == n_steps - 1)
    def _():
        for back in range(min(OUT_SLOTS, n_steps)):
            for cp in scatter(step - back, (step - back) % OUT_SLOTS):
                cp.wait()


def _ssm_out(yg, wglut, bglu, gssm, nb, nc):
    n_tblocks = SSM_CHUNK // TB
    const = lambda b, i: (0, 0)
    return pl.pallas_call(
        functools.partial(_ssm_out_kernel, nc=nc, n_tblocks=n_tblocks, n_steps=nb * n_tblocks),
        grid=(nb, n_tblocks),
        in_specs=[pl.BlockSpec((N_GROUPS, TB * SSM_GROUP, nc), lambda b, i: (0, i, b)),
                  pl.BlockSpec((D_SSM, D_SSM), const),
                  pl.BlockSpec((D_SSM, 1), const),
                  pl.BlockSpec((D_SSM, 1), const)],
        out_specs=pl.BlockSpec(memory_space=pl.ANY),
        out_shape=jax.ShapeDtypeStruct((nb, nc, SSM_CHUNK, D_SSM), F32),
        scratch_shapes=[pltpu.VMEM((OUT_SLOTS, TB, nc, D_SSM), F32),
                        pltpu.SemaphoreType.DMA((OUT_SLOTS,))],
        compiler_params=pltpu.CompilerParams(dimension_semantics=("arbitrary", "arbitrary"),
                                             vmem_limit_bytes=VMEM_LIMIT),
        name="ssm_out",
    )(yg, wglut, bglu, gssm)


def _build_bias_table(rb_ref, tab_scr):
    n_keys, width = KEY_BLOCKS * Q_BLOCK, REL_WIDTH
    s = lax.broadcasted_iota(jnp.int32, (1, width), 1)
    delta = jnp.where(s < Q_BLOCK, -s, width - s)
    rel = delta - (KEY_BLOCKS - 1) * Q_BLOCK
    idx = jnp.clip(rel, -MAX_REL, MAX_REL) + MAX_REL
    pick = (lax.broadcasted_iota(jnp.int32, (REL_PAD, width), 0) == idx).astype(F32)
    w = _dot(rb_ref[...], pick, HIGHEST)
    kk = lax.broadcasted_iota(jnp.int32, (n_keys, Q_BLOCK), 0)
    qq = lax.broadcasted_iota(jnp.int32, (n_keys, Q_BLOCK), 1)
    kc = kk // ATT_CHUNK - (KEY_BLOCKS - 1) * (Q_BLOCK // ATT_CHUNK)
    qc = qq // ATT_CHUNK
    valid = (kc <= qc) & (kc >= qc - LEFT_CHUNKS)
    for h in range(N_HEADS):
        rows = jnp.broadcast_to(w[h:h + 1, :], (n_keys, width))
        toep = pltpu.roll(rows, 0, 1, stride=1, stride_axis=0)
        tab_scr[h] = jnp.where(valid, toep[:, :Q_BLOCK] * LOG2E, MASK_VALUE)


def _ffn_stages(x1, gffn_ref, wg_ref, wu_ref, wd_ref):
    h2 = _rms_rows(x1, gffn_ref[...]).astype(BF16)
    acc, pending = None, None
    for c in range(FF_CHUNKS + 1):
        issued = None
        if c < FF_CHUNKS:
            cols = slice(c * FF_CHUNK, (c + 1) * FF_CHUNK)
            issued = (cols, _dot(h2, wg_ref[:, cols]), _dot(h2, wu_ref[:, cols]))
        if pending is not None:
            cols, gate, up = pending
            part = _dot((gate * _sigmoid(gate) * up).astype(BF16), wd_ref[cols, :])
            acc = part if acc is None else acc + part
        pending = issued
        yield acc


def _mix_ffn_kernel(q_ref, k0_ref, k1_ref, k2_ref, v0_ref, v1_ref, v2_ref, rb_ref, g_ref, x_ref, ys_ref, wo_ref,
                    gffn_ref, wg_ref, wu_ref, wd_ref, gfin_ref, o_ref, tab_scr, x1_scr, acc_scr, *, n_blocks):
    i = pl.program_id(1)
    cur, prev, prev2 = i % X1_SLOTS, (i + X1_SLOTS - 1) % X1_SLOTS, (i + X1_SLOTS - 2) % X1_SLOTS
    kv_all = ((k0_ref, v0_ref, 0), (k1_ref, v1_ref, 1), (k2_ref, v2_ref, 2))

    @pl.when((pl.program_id(0) == 0) & (i == 0))
    def _():
        _build_bias_table(rb_ref, tab_scr)

    def finish():
        o_ref[0] = _rms_rows(x1_scr[prev2] + acc_scr[...], gfin_ref[...])

    def body(kv_list, with_ffn, with_finish):
        ffn = _ffn_stages(x1_scr[prev], gffn_ref, wg_ref, wu_ref, wd_ref) if with_ffn else None
        acc = None
        if not kv_list:
            if with_finish:
                finish()
            if with_ffn:
                for acc in ffn:
                    pass
                acc_scr[...] = acc
            return
        q = q_ref[0]
        lane = lax.broadcasted_iota(jnp.int32, (Q_BLOCK, 2 * HEAD_DIM), 1)
        ssm_proj = _dot(ys_ref[0].astype(BF16), wo_ref[0:D_SSM, :])

        def scores(h):
            pair = slice((h // 2) * 2 * HEAD_DIM, (h // 2 + 1) * 2 * HEAD_DIM)
            qp = q[:, pair]
            qh = jnp.where((lane < HEAD_DIM) if h % 2 == 0 else (lane >= HEAD_DIM), qp, jnp.zeros_like(qp))
            return [_dot_nt(k_ref[0, :, pair], qh) for (k_ref, _, _) in kv_list]

        def probs(h, raw, half):
            cols = slice(half * HALF_Q, (half + 1) * HALF_Q)
            band_lo = half * HALF_Q
            band_hi = band_lo + (LEFT_CHUNKS + HALF_Q // ATT_CHUNK) * ATT_CHUNK
            parts = []
            for s, (_, _, rel) in zip(raw, kv_list):
                lo, hi = max(band_lo - rel * Q_BLOCK, 0), min(band_hi - rel * Q_BLOCK, Q_BLOCK)
                parts.append((lo, hi, s[lo:hi, cols] + tab_scr[h, rel * Q_BLOCK + lo:rel * Q_BLOCK + hi, cols]))
            m = parts[0][2].max(axis=0, keepdims=True)
            for _, _, s in parts[1:]:
                m = jnp.maximum(m, s.max(axis=0, keepdims=True))
            denom, out = None, []
            for lo, hi, s in parts:
                p = jnp.exp2(s - m)
                ps = p.sum(axis=0, keepdims=True)
                denom = ps if denom is None else denom + ps
                p = p.astype(BF16)
                pad = [jnp.zeros((n, HALF_Q), BF16) for n in (lo, Q_BLOCK - hi)]
                out.append(jnp.concatenate([z for z in (pad[0], p, pad[1]) if z.shape[0]], axis=0))
            return out, denom

        raw_next = scores(0)
        if with_finish:
            finish()
        heads, proj, ssq = [], None, None
        for h in range(N_HEADS):
            raw = raw_next
            if h + 1 < N_HEADS:
                raw_next = scores(h + 1)
            if with_ffn:
                for _ in range(FFN_STAGES_PER_HEAD[h]):
                    acc = next(ffn)
            (p_lo, d_lo), (p_hi, d_hi) = probs(h, raw, 0), probs(h, raw, 1)
            o = None
            for pl_, ph_, (_, vt_ref, _) in zip(p_lo, p_hi, kv_list):
                oj = _dot(vt_ref[h * HEAD_DIM:(h + 1) * HEAD_DIM, :], jnp.concatenate([pl_, ph_], axis=1))
                o = oj if o is None else o + oj
            heads.append(o / jnp.concatenate([d_lo, d_hi], axis=1))
            if len(heads) == HEADS_PER_PROJ:
                cols = slice((h + 1 - HEADS_PER_PROJ) * HEAD_DIM, (h + 1) * HEAD_DIM)
                ot = jnp.concatenate(heads, axis=0).T
                sq = jnp.sum(ot * ot, axis=1, keepdims=True)
                part = _dot((ot * g_ref[:, cols]).astype(BF16),
                            wo_ref[D_SSM + cols.start:D_SSM + cols.stop, :])
                proj, ssq = (part, sq) if proj is None else (proj + part, ssq + sq)
                heads = []
        if with_ffn:
            acc_scr[...] = acc
        x1_scr[cur] = x_ref[0] + ssm_proj + proj * lax.rsqrt(ssq / D_ATT + EPS)

    pl.when(i == 0)(lambda: body(kv_all[2:], False, False))
    pl.when(i == 1)(lambda: body(kv_all[1:], True, False))
    pl.when((i >= 2) & (i < n_blocks))(lambda: body(kv_all, True, True))
    pl.when(i == n_blocks)(lambda: body((), True, True))
    pl.when(i == n_blocks + 1)(lambda: body((), False, True))


def _mix_ffn(q, k, vt, rb_pad, g_row, x, ysn, wo, gffn, wg, wu, wd, gfin):
    nb, seq, _ = q.shape
    n_blocks = seq // Q_BLOCK
    blk = (1, Q_BLOCK, D_ATT)
    vblk = (D_ATT, Q_BLOCK)
    xblk = (1, Q_BLOCK, D_MODEL)
    const = lambda b, i: (0, 0)

    def resident(shape):
        return pl.BlockSpec(shape, const, pipeline_mode=pl.Buffered(1))

    def back(n):
        return lambda b, i: (b, jnp.maximum(jnp.minimum(i, n_blocks - 1) - n, 0), 0)

    def back_t(n):
        return lambda b, i: (0, b * n_blocks + jnp.maximum(jnp.minimum(i, n_blocks - 1) - n, 0))

    return pl.pallas_call(
        functools.partial(_mix_ffn_kernel, n_blocks=n_blocks),
        grid=(nb, n_blocks + 2),
        in_specs=[
            pl.BlockSpec(blk, back(0)),
            pl.BlockSpec(blk, back(2)), pl.BlockSpec(blk, back(1)), pl.BlockSpec(blk, back(0)),
            pl.BlockSpec(vblk, back_t(2)), pl.BlockSpec(vblk, back_t(1)), pl.BlockSpec(vblk, back_t(0)),
            pl.BlockSpec((N_HEADS, REL_PAD), const),
            pl.BlockSpec((1, D_ATT), const),
            pl.BlockSpec(xblk, back(0)),
            pl.BlockSpec(blk, back(0)),
            resident((D_MODEL, D_MODEL)),
            pl.BlockSpec((1, D_MODEL), const),
            resident((D_MODEL, D_FF)),
            resident((D_MODEL, D_FF)),
            resident((D_FF, D_MODEL)),
            pl.BlockSpec((1, D_MODEL), const),
        ],
        out_specs=pl.BlockSpec(xblk, lambda b, i: (b, jnp.maximum(i - 2, 0), 0)),
        out_shape=jax.ShapeDtypeStruct((nb, seq, D_MODEL), F32),
        scratch_shapes=[pltpu.VMEM((N_HEADS, KEY_BLOCKS * Q_BLOCK, Q_BLOCK), F32),
                        pltpu.VMEM((X1_SLOTS, Q_BLOCK, D_MODEL), F32),
                        pltpu.VMEM((Q_BLOCK, D_MODEL), F32)],
        compiler_params=pltpu.CompilerParams(dimension_semantics=("arbitrary", "arbitrary"),
                                             vmem_limit_bytes=VMEM_LIMIT),
        name="mix_ffn",
    )(q, k, k, k, vt, vt, vt, rb_pad, g_row, x, ysn, wo, gffn, wg, wu, wd, gfin)


def _twice(a):
    return jnp.concatenate([a, a], axis=-1)


def kernel(x, norm_mix_g, w_in, ssm_lam_re, ssm_lam_im, ssm_log_dt, ssm_b_re, ssm_b_im, ssm_c_re,
           ssm_c_im, ssm_d, ssm_w_glu, ssm_b_glu, attn_rel_bias, norm_ssm_out_g, norm_att_out_g,
           w_out, norm_ffn_g, w_gate, w_up, w_down, norm_final_g):
    nb, seq, _ = x.shape
    nc = seq // SSM_CHUNK
    assert w_in.shape[0] == 1 and seq % Q_BLOCK == 0 and (nb * seq) % ROW_TILE == 0
    w_in0 = w_in[0]
    scale = LOG2E / math.sqrt(HEAD_DIM)
    wqk = jnp.concatenate([w_in0[:, D_SSM:D_SSM + D_ATT] * scale, w_in0[:, D_SSM + D_ATT:D_SSM + 2 * D_ATT]],
                          axis=1).astype(BF16)
    wvt = w_in0[:, D_SSM + 2 * D_ATT:].T.astype(BF16)
    wut = w_in0[:, :D_SSM].T.astype(BF16)
    g_mix = norm_mix_g[0][None, :]
    x4 = x.reshape(nb, nc, SSM_CHUNK, D_MODEL)
    lam2 = _twice(jnp.stack([ssm_lam_re[0], ssm_lam_im[0]], axis=1))
    b_re2, b_im2 = _twice(ssm_b_re[0].transpose(0, 2, 1)), _twice(ssm_b_im[0].transpose(0, 2, 1))
    c_re2, c_im2 = _twice(ssm_c_re[0]), _twice(ssm_c_im[0])
    lag0 = (SSM_CHUNK - 1) * SSM_GROUP
    d_pad = jnp.pad(ssm_d[0], ((0, 0), (lag0, LAG_COLS - lag0 - SSM_GROUP)))[:, None, :]

    q, k, vt = _qkv(x.reshape(nb * seq, D_MODEL), g_mix, wqk, wvt)
    ut = _ssm_in(x4, g_mix, wut)
    yg = _ssm_core(ut, lam2, ssm_log_dt[0][:, None, None], b_re2, b_im2, c_re2, c_im2, d_pad, nc)
    ysn = _ssm_out(yg, ssm_w_glu[0].T.astype(BF16), ssm_b_glu[0][:, None], norm_ssm_out_g[0][:, None], nb, nc)
    return _mix_ffn(q.reshape(nb, seq, D_ATT), k.reshape(nb, seq, D_ATT), vt,
                    jnp.pad(attn_rel_bias[0], ((0, 0), (0, REL_PAD - (2 * MAX_REL + 1)))),
                    norm_att_out_g[0][None, :], x, ysn.reshape(nb, seq, D_SSM), w_out[0].astype(BF16),
                    norm_ffn_g[0][None, :], w_gate[0].astype(BF16), w_up[0].astype(BF16),
                    w_down[0].astype(BF16), norm_final_g[None, :])
```

```python
import functools
import math

import jax
import jax.numpy as jnp
from jax import lax
from jax.experimental import pallas as pl
from jax.experimental.pallas import tpu as pltpu

F32 = jnp.float32
BF16 = jnp.bfloat16
HIGHEST = lax.Precision.HIGHEST

D_MODEL = 1024
D_SSM = 512
D_ATT = 512
SSM_GROUP = 16
N_GROUPS = D_SSM // SSM_GROUP
SSM_STATE = 64
HEAD_DIM = 64
N_HEADS = D_ATT // HEAD_DIM
ATT_CHUNK = 64
LEFT_CHUNKS = 8
MAX_REL = 128
D_FF = 2816
EPS = 1e-6
MASK_VALUE = -1e30
LOG2E = math.log2(math.e)

SSM_CHUNK = 64
SSM_ROWS = SSM_CHUNK * SSM_GROUP
TOEP_TILE = 256
TOEP_T = TOEP_TILE // SSM_GROUP
TOEP_TILES = SSM_ROWS // TOEP_TILE
LAG_STEPS = SSM_CHUNK + TOEP_T
LAG_COLS = LAG_STEPS * SSM_GROUP
POW_ROWS = 72
GROUPS_PER_STEP = 4
Q_BLOCK = 256
HALF_Q = 128
HEADS_PER_PROJ = 4
KEY_BLOCKS = 3
REL_WIDTH = (KEY_BLOCKS + 1) * Q_BLOCK
REL_PAD = 384
ROW_TILE = 1024
TB = 8
FF_CHUNK = 256
FF_CHUNKS = D_FF // FF_CHUNK
SUBS = 2
STEP_ROWS = SUBS * Q_BLOCK
FFN_STAGES_PER_HEAD = (0,) + (1,) * 12 + (0, 0, 0)
X1_SLOTS = 3
IN_SLOTS = 2
OUT_SLOTS = 2
VMEM_LIMIT = 56 * 1024 * 1024


def _sigmoid(z):
    return 1.0 / (1.0 + jnp.exp(-z))


def _rms_rows(x, g_row):
    ms = jnp.mean(x * x, axis=-1, keepdims=True)
    return x * lax.rsqrt(ms + EPS) * g_row


def _dot(a, b, precision=None):
    return jnp.dot(a, b, preferred_element_type=F32, precision=precision)


def _dot_nt(a, b, precision=None):
    return lax.dot_general(a, b, (((1,), (1,)), ((), ())), preferred_element_type=F32, precision=precision)


def _strided_copies(hbm, buf, sem, step, slot, n_tblocks, to_hbm=False):
    b, i = step // n_tblocks, step % n_tblocks
    copies = []
    for tl in range(TB):
        src, dst = hbm.at[b, :, i * TB + tl, :], buf.at[slot, tl]
        if to_hbm:
            src, dst = dst, src
        copies.append(pltpu.make_async_copy(src, dst, sem.at[slot]))
    return copies


def _qkv_kernel(x_ref, g_ref, wqk_ref, wvt_ref, q_ref, k_ref, vt_ref):
    h = _rms_rows(x_ref[...], g_ref[...]).astype(BF16)
    qk = _dot(h, wqk_ref[...])
    q_ref[...] = qk[:, 0:D_ATT].astype(BF16)
    k_ref[...] = qk[:, D_ATT:2 * D_ATT].astype(BF16)
    vt_ref[...] = _dot_nt(wvt_ref[...], h).astype(BF16)


def _qkv(x2, g_row, wqk, wvt):
    n_rows = x2.shape[0]
    const = lambda i: (0, 0)
    out_shape = jax.ShapeDtypeStruct((n_rows, D_ATT), BF16)
    out_spec = pl.BlockSpec((ROW_TILE, D_ATT), lambda i: (i, 0))
    return pl.pallas_call(
        _qkv_kernel,
        grid=(n_rows // ROW_TILE,),
        in_specs=[pl.BlockSpec((ROW_TILE, D_MODEL), lambda i: (i, 0)),
                  pl.BlockSpec((1, D_MODEL), const),
                  pl.BlockSpec((D_MODEL, 2 * D_ATT), const),
                  pl.BlockSpec((D_ATT, D_MODEL), const)],
        out_specs=[out_spec, out_spec, pl.BlockSpec((D_ATT, ROW_TILE), lambda i: (0, i))],
        out_shape=[out_shape, out_shape, jax.ShapeDtypeStruct((D_ATT, n_rows), BF16)],
        compiler_params=pltpu.CompilerParams(dimension_semantics=("arbitrary",),
                                             vmem_limit_bytes=VMEM_LIMIT),
        name="qkv",
    )(x2, g_row, wqk, wvt)


def _ssm_in_kernel(x_hbm, g_ref, wut_ref, ut_ref, xbuf, h_scr, sem, *, nc, n_tblocks, n_steps):
    step = pl.program_id(0) * n_tblocks + pl.program_id(1)
    slot = step % IN_SLOTS

    @pl.when(step == 0)
    def _():
        for cp in _strided_copies(x_hbm, xbuf, sem, step, slot, n_tblocks):
            cp.start()

    @pl.when(step + 1 < n_steps)
    def _():
        for cp in _strided_copies(x_hbm, xbuf, sem, step + 1, (step + 1) % IN_SLOTS, n_tblocks):
            cp.start()

    for cp in _strided_copies(x_hbm, xbuf, sem, step, slot, n_tblocks):
        cp.wait()
    for tl in range(TB):
        h_scr[tl * nc:(tl + 1) * nc, :] = _rms_rows(xbuf[slot, tl], g_ref[...]).astype(BF16)
    ut = _dot_nt(wut_ref[...], h_scr[...])
    for tl in range(TB):
        ut_ref[:, tl * SSM_GROUP:(tl + 1) * SSM_GROUP, :] = (
            ut[:, tl * nc:(tl + 1) * nc].reshape(N_GROUPS, SSM_GROUP, nc).astype(BF16))


def _ssm_in(x4, g_row, wut):
    nb, nc = x4.shape[0], x4.shape[1]
    n_tblocks = SSM_CHUNK // TB
    return pl.pallas_call(
        functools.partial(_ssm_in_kernel, nc=nc, n_tblocks=n_tblocks, n_steps=nb * n_tblocks),
        grid=(nb, n_tblocks),
        in_specs=[pl.BlockSpec(memory_space=pl.ANY),
                  pl.BlockSpec((1, D_MODEL), lambda b, i: (0, 0)),
                  pl.BlockSpec((D_SSM, D_MODEL), lambda b, i: (0, 0))],
        out_specs=pl.BlockSpec((N_GROUPS, TB * SSM_GROUP, nc), lambda b, i: (0, i, b)),
        out_shape=jax.ShapeDtypeStruct((N_GROUPS, SSM_ROWS, nb * nc), BF16),
        scratch_shapes=[pltpu.VMEM((IN_SLOTS, TB, nc, D_MODEL), F32),
                        pltpu.VMEM((TB * nc, D_MODEL), BF16),
                        pltpu.SemaphoreType.DMA((IN_SLOTS,))],
        compiler_params=pltpu.CompilerParams(dimension_semantics=("arbitrary", "arbitrary"),
                                             vmem_limit_bytes=VMEM_LIMIT),
        name="ssm_in",
    )(x4, g_row, wut)


def _cexp(re, im):
    mag = jnp.exp(re)
    return mag * jnp.cos(im), mag * jnp.sin(im)


def _ssm_group(ut_ref, lam_ref, ldt_ref, bre_ref, bim_ref, cre_ref, cim_ref, d_ref,
               yg_ref, mt_scr, cpow_scr, tcol_scr, *, nc, n_steps):
    n_p, n_h = SSM_STATE, SSM_GROUP
    dt = jnp.exp(ldt_ref[0])
    lam_r, lam_i = lam_ref[0, 0:1, :], lam_ref[0, 1:2, :]
    ar, ai = lam_r * dt, lam_i * dt
    lbr, lbi = _cexp(ar, ai)
    den = lam_r * lam_r + lam_i * lam_i
    fr = ((lbr - 1.0) * lam_r + lbi * lam_i) / den
    fi = (lbi * lam_r - (lbr - 1.0) * lam_i) / den
    b_re, b_im = bre_ref[0], bim_ref[0]
    bbr, bbi = fr * b_re - fi * b_im, fr * b_im + fi * b_re
    c_re, c_im = cre_ref[0], cim_ref[0]
    n = 8
    jrow = lax.broadcasted_iota(jnp.int32, (n, 1), 0).astype(F32)
    pwr, pwi = _cexp(ar * jrow, ai * jrow)
    er, ei = _cexp(ar * float(n), ai * float(n))
    while n < POW_ROWS:
        take = min(n, POW_ROWS - n)
        pwr, pwi = (jnp.concatenate([pwr, pwr[:take] * er - pwi[:take] * ei], axis=0),
                    jnp.concatenate([pwi, pwr[:take] * ei + pwi[:take] * er], axis=0))
        er, ei = er * er - ei * ei, 2.0 * er * ei
        n += take
    re_half = lax.broadcasted_iota(jnp.int32, (n_h, 2 * n_p), 1) < n_p
    for m in range(SSM_CHUNK):
        j = SSM_CHUNK - 1 - m
        pr, pi = pwr[j:j + 1, :], pwi[j:j + 1, :]
        mt_scr[m * n_h:(m + 1) * n_h, :] = jnp.where(re_half, bbr * pr - bbi * pi, bbr * pi + bbi * pr)
    mt_scr[SSM_ROWS:, :] = jnp.zeros((LAG_COLS - SSM_ROWS, 2 * n_p), F32)
    for t in range(SSM_CHUNK):
        pr, pi = pwr[t + 1:t + 2, :], pwi[t + 1:t + 2, :]
        cpow_scr[t * n_h:(t + 1) * n_h, :] = jnp.where(
            re_half, c_re * pr - c_im * pi, -(c_re * pi + c_im * pr)).astype(BF16)
    yield
    mt = mt_scr[...]
    lagk = _dot_nt(jnp.where(re_half, c_re, -c_im), mt, HIGHEST)
    col_h = lax.broadcasted_iota(jnp.int32, (n_h, LAG_COLS), 1)
    row_h = lax.broadcasted_iota(jnp.int32, (n_h, LAG_COLS), 0)
    lagk = lagk + jnp.where(col_h % n_h == row_h, d_ref[0], 0.0)
    for t_out in range(SSM_CHUNK):
        start = (SSM_CHUNK - 1 - t_out) * n_h
        tcol_scr[t_out * n_h:(t_out + 1) * n_h, :] = lagk[:, start:start + TOEP_TILE].astype(BF16)
    u = ut_ref[0]
    lanes = u.shape[1]
    v = _dot(mt[:SSM_ROWS, :].T.astype(BF16), u)
    xr, xi = v[:n_p], v[n_p:]
    yield
    eye = (lax.broadcasted_iota(jnp.int32, (n_p, 2 * n_p), 0)
           == lax.broadcasted_iota(jnp.int32, (n_p, 2 * n_p), 1))
    a_r = jnp.sum(jnp.where(eye, pwr[SSM_CHUNK:SSM_CHUNK + 1, :], 0.0), axis=1, keepdims=True)
    a_i = jnp.sum(jnp.where(eye, pwi[SSM_CHUNK:SSM_CHUNK + 1, :], 0.0), axis=1, keepdims=True)
    chunk = lax.broadcasted_iota(jnp.int32, (n_p, lanes), 1) % nc
    for k in range(n_steps):
        sh = 1 << k
        keep = chunk >= sh
        sr = jnp.where(keep, pltpu.roll(xr, sh, 1), 0.0)
        si = jnp.where(keep, pltpu.roll(xi, sh, 1), 0.0)
        xr, xi = xr + (a_r * sr - a_i * si), xi + (a_r * si + a_i * sr)
        a_r, a_i = a_r * a_r - a_i * a_i, 2.0 * a_r * a_i
    keep = chunk >= 1
    pr = jnp.where(keep, pltpu.roll(xr, 1, 1), 0.0)
    pi = jnp.where(keep, pltpu.roll(xi, 1, 1), 0.0)
    xprev = jnp.concatenate([pr, pi], axis=0).astype(BF16)
    yield
    for bi in range(TOEP_TILES):
        rows = slice(bi * TOEP_TILE, (bi + 1) * TOEP_TILE)
        acc = _dot(cpow_scr[rows, :], xprev)
        for bj in range(bi + 1):
            d = bi - bj
            acc = acc + _dot(tcol_scr[d * TOEP_TILE:(d + 1) * TOEP_TILE, :],
                             u[bj * TOEP_TILE:(bj + 1) * TOEP_TILE, :])
        yg_ref[0, rows, :] = jax.nn.gelu(acc).astype(BF16)


def _ssm_kernel(*refs, nc, n_steps):
    n_io = 9
    groups = [_ssm_group(*[r.at[gi:gi + 1] for r in refs[:n_io]], *[s.at[gi] for s in refs[n_io:]],
                         nc=nc, n_steps=n_steps) for gi in range(GROUPS_PER_STEP)]
    while groups:
        groups = [g for g in groups if next(g, groups) is not groups]


def _ssm_core(ut, lam2, log_dt, b_re, b_im, c_re, c_im, d_pad, nc):
    lanes = ut.shape[2]
    n_steps = max(1, (nc - 1).bit_length())
    gps = GROUPS_PER_STEP
    g3 = lambda g: (g, 0, 0)
    par = pl.BlockSpec((gps, SSM_GROUP, 2 * SSM_STATE), g3)
    return pl.pallas_call(
        functools.partial(_ssm_kernel, nc=nc, n_steps=n_steps),
        grid=(N_GROUPS // gps,),
        in_specs=[
            pl.BlockSpec((gps, SSM_ROWS, lanes), g3),
            pl.BlockSpec((gps, 2, 2 * SSM_STATE), g3),
            pl.BlockSpec((gps, 1, 1), g3),
            par, par, par, par,
            pl.BlockSpec((gps, 1, LAG_COLS), g3),
        ],
        out_specs=pl.BlockSpec((gps, SSM_ROWS, lanes), g3),
        out_shape=jax.ShapeDtypeStruct((N_GROUPS, SSM_ROWS, lanes), BF16),
        scratch_shapes=[pltpu.VMEM((gps, LAG_COLS, 2 * SSM_STATE), F32),
                        pltpu.VMEM((gps, SSM_ROWS, 2 * SSM_STATE), BF16),
                        pltpu.VMEM((gps, SSM_ROWS, TOEP_TILE), BF16)],
        compiler_params=pltpu.CompilerParams(dimension_semantics=("arbitrary",),
                                             vmem_limit_bytes=VMEM_LIMIT),
        name="ssm_core",
    )(ut, lam2, log_dt, b_re, b_im, c_re, c_im, d_pad)


def _ssm_out_kernel(yg_ref, wglut_ref, bglu_ref, gssm_ref, o_hbm, ybuf, sem, *, nc, n_tblocks, n_steps):
    step = pl.program_id(0) * n_tblocks + pl.program_id(1)
    slot = step % OUT_SLOTS
    scatter = functools.partial(_strided_copies, o_hbm, ybuf, sem, n_tblocks=n_tblocks, to_hbm=True)

    @pl.when(step >= OUT_SLOTS)
    def _():
        for cp in scatter(step - OUT_SLOTS, slot):
            cp.wait()

    for tl in range(TB):
        yg = yg_ref[:, tl * SSM_GROUP:(tl + 1) * SSM_GROUP, :].reshape(D_SSM, nc)
        z = _dot(wglut_ref[...], yg) + bglu_ref[...]
        y2 = yg.astype(F32) * _sigmoid(z)
        ms = jnp.mean(y2 * y2, axis=0, keepdims=True)
        ybuf[slot, tl] = (y2 * lax.rsqrt(ms + EPS) * gssm_ref[...]).T
    for cp in scatter(step, slot):
        cp.start()

    @pl.when(step == n_steps - 1)
    def _():
        for back in range(min(OUT_SLOTS, n_steps)):
            for cp in scatter(step - back, (step - back) % OUT_SLOTS):
                cp.wait()


def _ssm_out(yg, wglut, bglu, gssm, nb, nc):
    n_tblocks = SSM_CHUNK // TB
    const = lambda b, i: (0, 0)
    return pl.pallas_call(
        functools.partial(_ssm_out_kernel, nc=nc, n_tblocks=n_tblocks, n_steps=nb * n_tblocks),
        grid=(nb, n_tblocks),
        in_specs=[pl.BlockSpec((N_GROUPS, TB * SSM_GROUP, nc), lambda b, i: (0, i, b)),
                  pl.BlockSpec((D_SSM, D_SSM), const),
                  pl.BlockSpec((D_SSM, 1), const),
                  pl.BlockSpec((D_SSM, 1), const)],
        out_specs=pl.BlockSpec(memory_space=pl.ANY),
        out_shape=jax.ShapeDtypeStruct((nb, nc, SSM_CHUNK, D_SSM), F32),
        scratch_shapes=[pltpu.VMEM((OUT_SLOTS, TB, nc, D_SSM), F32),
                        pltpu.SemaphoreType.DMA((OUT_SLOTS,))],
        compiler_params=pltpu.CompilerParams(dimension_semantics=("arbitrary", "arbitrary"),
                                             vmem_limit_bytes=VMEM_LIMIT),
        name="ssm_out",
    )(yg, wglut, bglu, gssm)


def _build_bias_table(rb_ref, tab_scr):
    n_keys, width = KEY_BLOCKS * Q_BLOCK, REL_WIDTH
    s = lax.broadcasted_iota(jnp.int32, (1, width), 1)
    delta = jnp.where(s < Q_BLOCK, -s, width - s)
    rel = delta - (KEY_BLOCKS - 1) * Q_BLOCK
    idx = jnp.clip(rel, -MAX_REL, MAX_REL) + MAX_REL
    pick = (lax.broadcasted_iota(jnp.int32, (REL_PAD, width), 0) == idx).astype(F32)
    w = _dot(rb_ref[...], pick, HIGHEST)
    kk = lax.broadcasted_iota(jnp.int32, (n_keys, Q_BLOCK), 0)
    qq = lax.broadcasted_iota(jnp.int32, (n_keys, Q_BLOCK), 1)
    kc = kk // ATT_CHUNK - (KEY_BLOCKS - 1) * (Q_BLOCK // ATT_CHUNK)
    qc = qq // ATT_CHUNK
    valid = (kc <= qc) & (kc >= qc - LEFT_CHUNKS)
    for h in range(N_HEADS):
        rows = jnp.broadcast_to(w[h:h + 1, :], (n_keys, width))
        toep = pltpu.roll(rows, 0, 1, stride=1, stride_axis=0)
        tab_scr[h] = jnp.where(valid, toep[:, :Q_BLOCK] * LOG2E, MASK_VALUE)


def _ffn_stages(x1, gffn_ref, wg_ref, wu_ref, wd_ref):
    h2 = _rms_rows(x1, gffn_ref[...]).astype(BF16)
    acc, pending = None, None
    for c in range(FF_CHUNKS + 1):
        issued = None
        if c < FF_CHUNKS:
            cols = slice(c * FF_CHUNK, (c + 1) * FF_CHUNK)
            issued = (cols, _dot(h2, wg_ref[:, cols]), _dot(h2, wu_ref[:, cols]))
        if pending is not None:
            cols, gate, up = pending
            part = _dot((gate * _sigmoid(gate) * up).astype(BF16), wd_ref[cols, :])
            acc = part if acc is None else acc + part
        pending = issued
        yield acc


def _mix_ffn_kernel(q_ref, kp_ref, kc_ref, vp_ref, vc_ref, rb_ref, g_ref, x_ref, ys_ref, wo_ref,
                    gffn_ref, wg_ref, wu_ref, wd_ref, gfin_ref, o_ref, tab_scr, x1_scr, acc_scr, *, n_steps):
    j = pl.program_id(1)
    cur, prev, prev2 = j % X1_SLOTS, (j + X1_SLOTS - 1) % X1_SLOTS, (j + X1_SLOTS - 2) % X1_SLOTS

    @pl.when((pl.program_id(0) == 0) & (j == 0))
    def _():
        _build_bias_table(rb_ref, tab_scr)

    def finish():
        o_ref[0] = _rms_rows(x1_scr[prev2] + acc_scr[...], gfin_ref[...])

    def body(first, with_attn, with_ffn, with_finish):
        ffn = _ffn_stages(x1_scr[prev], gffn_ref, wg_ref, wu_ref, wd_ref) if with_ffn else None
        acc = None
        if with_finish:
            finish()
        if not with_attn:
            if with_ffn:
                for acc in ffn:
                    pass
                acc_scr[...] = acc
            return
        lane = lax.broadcasted_iota(jnp.int32, (Q_BLOCK, 2 * HEAD_DIM), 1)
        ssm_proj = _dot(ys_ref[0].astype(BF16), wo_ref[0:D_SSM, :])

        window = [(kp_ref, vp_ref, 0), (kp_ref, vp_ref, Q_BLOCK), (kc_ref, vc_ref, 0), (kc_ref, vc_ref, Q_BLOCK)]
        slots = []
        for sub in range(SUBS):
            kv = [(k, v, off, rel) for rel, (k, v, off) in enumerate(window[sub:sub + KEY_BLOCKS])]
            if first:
                kv = [e for e in kv if e[0] is kc_ref]
            slots += [(sub, kv, h) for h in range(N_HEADS)]

        def scores(slot):
            sub, kv, h = slot
            pair = slice((h // 2) * 2 * HEAD_DIM, (h // 2 + 1) * 2 * HEAD_DIM)
            qp = q_ref[0, sub * Q_BLOCK:(sub + 1) * Q_BLOCK, pair]
            qh = jnp.where((lane < HEAD_DIM) if h % 2 == 0 else (lane >= HEAD_DIM), qp, jnp.zeros_like(qp))
            return [_dot_nt(k_ref[0, off:off + Q_BLOCK, pair], qh) for (k_ref, _, off, _) in kv]

        def probs(slot, raw, half):
            _, kv, h = slot
            cols = slice(half * HALF_Q, (half + 1) * HALF_Q)
            band_lo = half * HALF_Q
            band_hi = band_lo + (LEFT_CHUNKS + HALF_Q // ATT_CHUNK) * ATT_CHUNK
            parts = []
            for s, (_, _, _, rel) in zip(raw, kv):
                lo, hi = max(band_lo - rel * Q_BLOCK, 0), min(band_hi - rel * Q_BLOCK, Q_BLOCK)
                parts.append((lo, hi, s[lo:hi, cols] + tab_scr[h, rel * Q_BLOCK + lo:rel * Q_BLOCK + hi, cols]))
            m = parts[0][2].max(axis=0, keepdims=True)
            for _, _, s in parts[1:]:
                m = jnp.maximum(m, s.max(axis=0, keepdims=True))
            denom, out = None, []
            for lo, hi, s in parts:
                p = jnp.exp2(s - m)
                ps = p.sum(axis=0, keepdims=True)
                denom = ps if denom is None else denom + ps
                p = p.astype(BF16)
                pad = [jnp.zeros((n, HALF_Q), BF16) for n in (lo, Q_BLOCK - hi)]
                out.append(jnp.concatenate([z for z in (pad[0], p, pad[1]) if z.shape[0]], axis=0))
            return out, denom

        raw_next = scores(slots[0])
        heads, proj, ssq = [], None, None
        for n, slot in enumerate(slots):
            sub, kv, h = slot
            raw = raw_next
            if n + 1 < len(slots):
                raw_next = scores(slots[n + 1])
            if with_ffn and FFN_STAGES_PER_HEAD[n]:
                for _ in range(FFN_STAGES_PER_HEAD[n]):
                    acc = next(ffn)
                if sum(FFN_STAGES_PER_HEAD[:n + 1]) == FF_CHUNKS + 1:
                    acc_scr[...] = acc
            (p_lo, d_lo), (p_hi, d_hi) = probs(slot, raw, 0), probs(slot, raw, 1)
            o = None
            for pl_, ph_, (_, vt_ref, off, _) in zip(p_lo, p_hi, kv):
                oj = _dot(vt_ref[h * HEAD_DIM:(h + 1) * HEAD_DIM, off:off + Q_BLOCK],
                          jnp.concatenate([pl_, ph_], axis=1))
                o = oj if o is None else o + oj
            heads.append(o / jnp.concatenate([d_lo, d_hi], axis=1))
            if len(heads) == HEADS_PER_PROJ:
                cols = slice((h + 1 - HEADS_PER_PROJ) * HEAD_DIM, (h + 1) * HEAD_DIM)
                ot = jnp.concatenate(heads, axis=0).T
                sq = jnp.sum(ot * ot, axis=1, keepdims=True)
                part = _dot((ot * g_ref[:, cols]).astype(BF16),
                            wo_ref[D_SSM + cols.start:D_SSM + cols.stop, :])
                proj, ssq = (part, sq) if proj is None else (proj + part, ssq + sq)
                heads = []
            if h == N_HEADS - 1:
                rows = slice(sub * Q_BLOCK, (sub + 1) * Q_BLOCK)
                x1_scr[cur, rows, :] = x_ref[0, rows, :] + ssm_proj[rows] + proj * lax.rsqrt(ssq / D_ATT + EPS)
                proj, ssq = None, None

    n_att = n_steps - 2
    pl.when(j == 0)(lambda: body(True, True, False, False))
    pl.when(j == 1)(lambda: body(False, True, True, False))
    pl.when((j >= 2) & (j < n_att))(lambda: body(False, True, True, True))
    pl.when(j == n_att)(lambda: body(False, False, True, True))
    pl.when(j == n_att + 1)(lambda: body(False, False, False, True))


def _mix_ffn(q, k, vt, rb_pad, g_row, x, ysn, wo, gffn, wg, wu, wd, gfin):
    nb, seq, _ = q.shape
    n_att = seq // STEP_ROWS
    blk = (1, STEP_ROWS, D_ATT)
    vblk = (D_ATT, STEP_ROWS)
    xblk = (1, STEP_ROWS, D_MODEL)
    const = lambda b, j: (0, 0)

    def resident(shape):
        return pl.BlockSpec(shape, const, pipeline_mode=pl.Buffered(1))

    def back(n):
        return lambda b, j: (b, jnp.maximum(jnp.minimum(j, n_att - 1) - n, 0), 0)

    def back_t(n):
        return lambda b, j: (0, b * n_att + jnp.maximum(jnp.minimum(j, n_att - 1) - n, 0))

    return pl.pallas_call(
        functools.partial(_mix_ffn_kernel, n_steps=n_att + 2),
        grid=(nb, n_att + 2),
        in_specs=[
            pl.BlockSpec(blk, back(0)),
            pl.BlockSpec(blk, back(1)), pl.BlockSpec(blk, back(0)),
            pl.BlockSpec(vblk, back_t(1)), pl.BlockSpec(vblk, back_t(0)),
            pl.BlockSpec((N_HEADS, REL_PAD), const),
            pl.BlockSpec((1, D_ATT), const),
            pl.BlockSpec(xblk, back(0)),
            pl.BlockSpec(blk, back(0)),
            resident((D_MODEL, D_MODEL)),
            pl.BlockSpec((1, D_MODEL), const),
            resident((D_MODEL, D_FF)),
            resident((D_MODEL, D_FF)),
            resident((D_FF, D_MODEL)),
            pl.BlockSpec((1, D_MODEL), const),
        ],
        out_specs=pl.BlockSpec(xblk, lambda b, j: (b, jnp.maximum(j - 2, 0), 0)),
        out_shape=jax.ShapeDtypeStruct((nb, seq, D_MODEL), F32),
        scratch_shapes=[pltpu.VMEM((N_HEADS, KEY_BLOCKS * Q_BLOCK, Q_BLOCK), F32),
                        pltpu.VMEM((X1_SLOTS, STEP_ROWS, D_MODEL), F32),
                        pltpu.VMEM((STEP_ROWS, D_MODEL), F32)],
        compiler_params=pltpu.CompilerParams(dimension_semantics=("arbitrary", "arbitrary"),
                                             vmem_limit_bytes=VMEM_LIMIT),
        name="mix_ffn",
    )(q, k, k, vt, vt, rb_pad, g_row, x, ysn, wo, gffn, wg, wu, wd, gfin)


def _twice(a):
    return jnp.concatenate([a, a], axis=-1)


def kernel(x, norm_mix_g, w_in, ssm_lam_re, ssm_lam_im, ssm_log_dt, ssm_b_re, ssm_b_im, ssm_c_re,
           ssm_c_im, ssm_d, ssm_w_glu, ssm_b_glu, attn_rel_bias, norm_ssm_out_g, norm_att_out_g,
           w_out, norm_ffn_g, w_gate, w_up, w_down, norm_final_g):
    nb, seq, _ = x.shape
    nc = seq // SSM_CHUNK
    assert w_in.shape[0] == 1 and seq % STEP_ROWS == 0 and (nb * seq) % ROW_TILE == 0
    w_in0 = w_in[0]
    scale = LOG2E / math.sqrt(HEAD_DIM)
    wqk = jnp.concatenate([w_in0[:, D_SSM:D_SSM + D_ATT] * scale, w_in0[:, D_SSM + D_ATT:D_SSM + 2 * D_ATT]],
                          axis=1).astype(BF16)
    wvt = w_in0[:, D_SSM + 2 * D_ATT:].T.astype(BF16)
    wut = w_in0[:, :D_SSM].T.astype(BF16)
    g_mix = norm_mix_g[0][None, :]
    x4 = x.reshape(nb, nc, SSM_CHUNK, D_MODEL)
    lam2 = _twice(jnp.stack([ssm_lam_re[0], ssm_lam_im[0]], axis=1))
    b_re2, b_im2 = _twice(ssm_b_re[0].transpose(0, 2, 1)), _twice(ssm_b_im[0].transpose(0, 2, 1))
    c_re2, c_im2 = _twice(ssm_c_re[0]), _twice(ssm_c_im[0])
    lag0 = (SSM_CHUNK - 1) * SSM_GROUP
    d_pad = jnp.pad(ssm_d[0], ((0, 0), (lag0, LAG_COLS - lag0 - SSM_GROUP)))[:, None, :]

    q, k, vt = _qkv(x.reshape(nb * seq, D_MODEL), g_mix, wqk, wvt)
    ut = _ssm_in(x4, g_mix, wut)
    yg = _ssm_core(ut, lam2, ssm_log_dt[0][:, None, None], b_re2, b_im2, c_re2, c_im2, d_pad, nc)
    ysn = _ssm_out(yg, ssm_w_glu[0].T.astype(BF16), ssm_b_glu[0][:, None], norm_ssm_out_g[0][:, None], nb, nc)
    return _mix_ffn(q.reshape(nb, seq, D_ATT), k.reshape(nb, seq, D_ATT), vt,
                    jnp.pad(attn_rel_bias[0], ((0, 0), (0, REL_PAD - (2 * MAX_REL + 1)))),
                    norm_att_out_g[0][None, :], x, ysn.reshape(nb, seq, D_SSM), w_out[0].astype(BF16),
                    norm_ffn_g[0][None, :], w_gate[0].astype(BF16), w_up[0].astype(BF16),
                    w_down[0].astype(BF16), norm_final_g[None, :])
```
